```python
import jax, jax.numpy as jnp
from jax import lax
import numpy as np

D_MODEL = 2048
BATCH = 1
SEQ = 8192
DEPTH = 1
DEC_BATCH = 128
DEC_SEQ = 4
PAST_LEN = 16384
PAGE_SIZE = 128

MLA_HEADS = 8
Q_LORA = 512
KV_LORA = 256
NOPE_DIM = 128
ROPE_DIM = 64
QK_DIM_MLA = NOPE_DIM + ROPE_DIM
V_DIM_MLA = 128
FOX_HEADS = 4
FOX_DIM = 128
MEM_HEADS = 4
MEM_DIM = 128
MEM_LEN = 256
N_BRANCH = 3
N_GROUPS = 8
EXP_PER_GROUP = 8
N_EXPERTS = N_GROUPS * EXP_PER_GROUP
TOP_K = 2
D_FF_EXPERT = 512

ROPE_THETA = 10000.0
EPS = 1e-6
Q_BLOCK = 128
NEG_INF = -1e30
MLA_SCALE = QK_DIM_MLA ** -0.5
FOX_SCALE = FOX_DIM ** -0.5
MEM_SCALE = MEM_DIM ** -0.5
F32 = jnp.float32
IN_WIDTHS = (Q_LORA, KV_LORA, ROPE_DIM, FOX_HEADS * FOX_DIM, FOX_HEADS * FOX_DIM, FOX_HEADS * FOX_DIM,
             FOX_HEADS, MEM_HEADS * MEM_DIM, N_BRANCH * D_MODEL)
D_IN = sum(IN_WIDTHS)

kernel_name = 'hybrid_mla_fox_memory_hmoe_step'


def rms_norm(x, g):
    xf = x.astype(F32)
    y = xf * lax.rsqrt(jnp.mean(xf * xf, axis=-1, keepdims=True) + EPS)
    return (y * g.astype(F32)).astype(x.dtype)


def rope(x, pos):
    half = ROPE_DIM // 2
    inv = ROPE_THETA ** (-jnp.arange(half, dtype=F32) / half)
    ang = pos.astype(F32)[:, None] * inv[None, :]
    shape = (ang.shape[0],) + (1,) * (x.ndim - 3) + (half,)
    cos, sin = jnp.cos(ang).reshape(shape), jnp.sin(ang).reshape(shape)
    xf = x.astype(F32)
    x1, x2 = xf[..., :half], xf[..., half:]
    return jnp.concatenate([x1 * cos - x2 * sin, x1 * sin + x2 * cos], axis=-1).astype(x.dtype)


def split_projection(proj):
    parts, start = [], 0
    for w in IN_WIDTHS:
        parts.append(proj[..., start:start + w])
        start += w
    return parts


def token_projections(xn, pos, w_in, b_f, g_cq, w_uq, g_ckv, g_q_mla, g_q_fox, g_k_fox, g_q_mem):
    B, T, _ = xn.shape
    c_q, c_kv, k_r, f_q, f_k, f_v, f_logit, m_q, gate_logit = split_projection(xn @ w_in)
    q = (rms_norm(c_q, g_cq) @ w_uq).reshape(B, T, MLA_HEADS, QK_DIM_MLA)
    q_mla = rms_norm(jnp.concatenate([q[..., :NOPE_DIM], rope(q[..., NOPE_DIM:], pos)], axis=-1), g_q_mla)
    latent = rms_norm(c_kv, g_ckv)
    k_rope = rope(k_r, pos)
    q_fox = rms_norm(f_q.reshape(B, T, FOX_HEADS, FOX_DIM), g_q_fox)
    k_fox = rms_norm(f_k.reshape(B, T, FOX_HEADS, FOX_DIM), g_k_fox)
    v_fox = f_v.reshape(B, T, FOX_HEADS, FOX_DIM)
    log_f = jax.nn.log_sigmoid(f_logit.astype(F32) + b_f.astype(F32))
    q_mem = rms_norm(m_q.reshape(B, T, MEM_HEADS, MEM_DIM), g_q_mem)
    gates = jax.nn.sigmoid(gate_logit.astype(F32)).astype(xn.dtype).reshape(B, T, N_BRANCH, D_MODEL)
    return q_mla, latent, k_rope, q_fox, k_fox, v_fox, log_f, q_mem, gates


def mla_keys(latent, k_rope, w_uk, g_k_mla):
    k_nope = jnp.einsum('bkc,chd->bkhd', latent, w_uk)
    k_r = jnp.broadcast_to(k_rope[:, :, None, :], k_nope.shape[:3] + (ROPE_DIM,))
    return rms_norm(jnp.concatenate([k_nope, k_r], axis=-1), g_k_mla)


def memory_kv(mem, g_mem_norm, w_mem_kv, g_k_mem):
    B, M, _ = mem.shape
    kv = rms_norm(mem, g_mem_norm) @ w_mem_kv
    k = rms_norm(kv[..., :MEM_HEADS * MEM_DIM].reshape(B, M, MEM_HEADS, MEM_DIM), g_k_mem)
    v = kv[..., MEM_HEADS * MEM_DIM:].reshape(B, M, MEM_HEADS, MEM_DIM)
    return k, v


def memory_attention(q, k, v):
    B, T = q.shape[:2]
    s = jnp.einsum('bthd,bmhd->bhtm', q, k).astype(F32) * MEM_SCALE
    p = jax.nn.softmax(s, axis=-1)
    return jnp.einsum('bhtm,bmhd->bthd', p.astype(v.dtype), v).reshape(B, T, MEM_HEADS * MEM_DIM)


def prompt_attention(q_mla, k_mla, v_mla, q_fox, k_fox, v_fox, log_f):
    B, T = q_mla.shape[:2]
    F = jnp.cumsum(log_f, axis=1).transpose(0, 2, 1)
    k_pos = jnp.arange(T)

    def block(i):
        s0 = i * Q_BLOCK
        causal = k_pos[None, :] <= (s0 + jnp.arange(Q_BLOCK))[:, None]
        qm = lax.dynamic_slice_in_dim(q_mla, s0, Q_BLOCK, axis=1)
        s_m = jnp.einsum('bqhd,bkhd->bhqk', qm, k_mla).astype(F32) * MLA_SCALE
        p_m = jax.nn.softmax(jnp.where(causal, s_m, NEG_INF), axis=-1)
        o_m = jnp.einsum('bhqk,bkhd->bqhd', p_m.astype(v_mla.dtype), v_mla)
        qf = lax.dynamic_slice_in_dim(q_fox, s0, Q_BLOCK, axis=1)
        Fq = lax.dynamic_slice_in_dim(F, s0, Q_BLOCK, axis=2)
        s_f = (jnp.einsum('bqhd,bkhd->bhqk', qf, k_fox).astype(F32) * FOX_SCALE
               + Fq[..., :, None] - F[..., None, :])
        p_f = jax.nn.softmax(jnp.where(causal, s_f, NEG_INF), axis=-1)
        o_f = jnp.einsum('bhqk,bkhd->bqhd', p_f.astype(v_fox.dtype), v_fox)
        return o_m, o_f

    o_m, o_f = lax.map(block, jnp.arange(T // Q_BLOCK))
    o_m = jnp.moveaxis(o_m, 0, 1).reshape(B, T, MLA_HEADS * V_DIM_MLA)
    o_f = jnp.moveaxis(o_f, 0, 1).reshape(B, T, FOX_HEADS * FOX_DIM)
    return o_m, o_f


def online_softmax_step(carry, s, vals, eq):
    m, l, acc = carry
    m_new = jnp.maximum(m, jnp.max(s, axis=-1))
    alpha = jnp.exp(m - m_new)
    p = jnp.exp(s - m_new[..., None])
    return (m_new, l * alpha + jnp.sum(p, axis=-1),
            acc * alpha[..., None] + jnp.einsum(eq, p, vals.astype(F32)))


def sample_attention(q_mla, lat_new, kr_new, q_fox, k_fox_new, v_fox_new, logf_new,
                     cache_mla_latent, cache_mla_krope, cache_fox_k, cache_fox_v, cache_fox_logf,
                     layer, page_table, w_uk, w_uv, g_k_mla):
    DB, T = q_mla.shape[:2]
    n_pages = page_table.shape[1]
    logf_past = cache_fox_logf[layer, page_table].astype(F32).reshape(DB, n_pages * PAGE_SIZE, FOX_HEADS)
    after = lax.cumsum(logf_past, axis=1, reverse=True) - logf_past
    after = after.reshape(DB, n_pages, PAGE_SIZE, FOX_HEADS).transpose(1, 0, 3, 2)
    c_new = jnp.cumsum(logf_new, axis=1).transpose(0, 2, 1)

    def init(h, d):
        return (jnp.full((DB, h, T), NEG_INF, F32), jnp.zeros((DB, h, T), F32), jnp.zeros((DB, h, T, d), F32))

    def body(carry, xs):
        c_mla, c_fox = carry
        pids, after_blk = xs
        lat = cache_mla_latent[layer, pids]
        k_m = mla_keys(lat, cache_mla_krope[layer, pids], w_uk, g_k_mla)
        s_m = jnp.einsum('bqhd,bkhd->bhqk', q_mla, k_m).astype(F32) * MLA_SCALE
        c_mla = online_softmax_step(c_mla, s_m, lat, 'bhqk,bkc->bhqc')
        s_f = (jnp.einsum('bqhd,bkhd->bhqk', q_fox, cache_fox_k[layer, pids]).astype(F32) * FOX_SCALE
               + c_new[..., :, None] + after_blk[:, :, None, :])
        c_fox = online_softmax_step(c_fox, s_f, cache_fox_v[layer, pids], 'bhqk,bkhd->bhqd')
        return (c_mla, c_fox), None

    (c_mla, c_fox), _ = lax.scan(body, (init(MLA_HEADS, KV_LORA), init(FOX_HEADS, FOX_DIM)),
                                 (page_table.T, after))
    causal = jnp.tril(jnp.ones((T, T), dtype=bool))
    k_m = mla_keys(lat_new, kr_new, w_uk, g_k_mla)
    s_m = jnp.einsum('bqhd,bkhd->bhqk', q_mla, k_m).astype(F32) * MLA_SCALE
    c_mla = online_softmax_step(c_mla, jnp.where(causal, s_m, NEG_INF), lat_new, 'bhqk,bkc->bhqc')
    s_f = (jnp.einsum('bqhd,bkhd->bhqk', q_fox, k_fox_new).astype(F32) * FOX_SCALE
           + c_new[..., :, None] - c_new[..., None, :])
    c_fox = online_softmax_step(c_fox, jnp.where(causal, s_f, NEG_INF), v_fox_new, 'bhqk,bkhd->bhqd')
    _, l_m, acc_m = c_mla
    o_m = jnp.einsum('bhqc,chd->bqhd', acc_m / l_m[..., None], w_uv).astype(v_fox_new.dtype)
    _, l_f, acc_f = c_fox
    o_f = (acc_f / l_f[..., None]).transpose(0, 2, 1, 3).astype(v_fox_new.dtype)
    return o_m.reshape(DB, T, MLA_HEADS * V_DIM_MLA), o_f.reshape(DB, T, FOX_HEADS * FOX_DIM)


def hier_moe(x, w_rg, b_rg, w_re, b_re, w_e_gate, w_e_up, w_e_down):
    B, T, D = x.shape
    xt = x.reshape(B * T, D)
    p_group = jax.nn.softmax((xt @ w_rg).astype(F32) + b_rg.astype(F32), axis=-1)
    p_g, g_idx = lax.top_k(p_group, 1)
    e_logits = ((xt @ w_re).astype(F32) + b_re.astype(F32)).reshape(-1, N_GROUPS, EXP_PER_GROUP)
    e_logits = jnp.take_along_axis(e_logits, g_idx[:, :, None], axis=1)[:, 0]
    p_e, e_idx = lax.top_k(jax.nn.softmax(e_logits, axis=-1), TOP_K)
    w_tok = p_g * p_e / jnp.sum(p_e, axis=-1, keepdims=True)
    combine = jnp.sum(jax.nn.one_hot(g_idx * EXP_PER_GROUP + e_idx, N_EXPERTS, dtype=F32)
                      * w_tok[..., None], axis=1)
    y = jnp.zeros((B * T, D), F32)
    for g in range(N_GROUPS):
        sl = slice(g * EXP_PER_GROUP, (g + 1) * EXP_PER_GROUP)
        h = (jax.nn.silu(jnp.einsum('nd,edf->nef', xt, w_e_gate[sl]))
             * jnp.einsum('nd,edf->nef', xt, w_e_up[sl]))
        y = y + jnp.einsum('nef,efd->nd', h * combine[:, sl, None], w_e_down[sl])
    return y.astype(x.dtype).reshape(B, T, D)


def merge_and_ffn(x, o_mla, o_fox, o_mem, gates, w_br_mla, w_br_fox, w_br_mem, w_o,
                  g_ffn_norm, w_rg, b_rg, w_re, b_re, w_e_gate, w_e_up, w_e_down):
    merged = (gates[..., 0, :] * (o_mla @ w_br_mla) + gates[..., 1, :] * (o_fox @ w_br_fox)
              + gates[..., 2, :] * (o_mem @ w_br_mem))
    h = x + merged @ w_o
    return h + hier_moe(rms_norm(h, g_ffn_norm), w_rg, b_rg, w_re, b_re, w_e_gate, w_e_up, w_e_down)


def decoder_layer(x_p, x_s, mem_p, cache_mla_latent, cache_mla_krope, cache_fox_k, cache_fox_v,
                  cache_fox_logf, c_mem_k, c_mem_v, layer, page_table,
                  g_attn_norm, w_in, b_f, g_cq, w_uq, g_ckv, w_uk, w_uv, g_q_mla, g_k_mla,
                  g_q_fox, g_k_fox, g_mem_norm, w_mem_kv, g_q_mem, g_k_mem,
                  w_br_mla, w_br_fox, w_br_mem, w_o, g_ffn_norm, w_rg, b_rg, w_re, b_re,
                  w_e_gate, w_e_up, w_e_down):
    proj_w = (w_in, b_f, g_cq, w_uq, g_ckv, g_q_mla, g_q_fox, g_k_fox, g_q_mem)
    ffn_w = (w_br_mla, w_br_fox, w_br_mem, w_o, g_ffn_norm, w_rg, b_rg, w_re, b_re, w_e_gate, w_e_up, w_e_down)
    pos_p = jnp.arange(x_p.shape[1])
    q_mla, lat, kr, q_f, k_f, v_f, logf, q_m, gates = token_projections(rms_norm(x_p, g_attn_norm), pos_p, *proj_w)
    k_mla = mla_keys(lat, kr, w_uk, g_k_mla)
    v_mla = jnp.einsum('btc,chd->bthd', lat, w_uv)
    o_mla, o_fox = prompt_attention(q_mla, k_mla, v_mla, q_f, k_f, v_f, logf)
    mem_k, mem_v = memory_kv(mem_p, g_mem_norm, w_mem_kv, g_k_mem)
    o_mem = memory_attention(q_m, mem_k, mem_v)
    y_p = merge_and_ffn(x_p, o_mla, o_fox, o_mem, gates, *ffn_w)
    past_len = page_table.shape[1] * PAGE_SIZE
    pos_s = past_len + jnp.arange(x_s.shape[1])
    s_q_mla, s_lat, s_kr, s_qf, s_kf, s_vf, s_logf, s_qm, s_gates = token_projections(
        rms_norm(x_s, g_attn_norm), pos_s, *proj_w)
    so_mla, so_fox = sample_attention(s_q_mla, s_lat, s_kr, s_qf, s_kf, s_vf, s_logf,
                                      cache_mla_latent, cache_mla_krope, cache_fox_k, cache_fox_v,
                                      cache_fox_logf, layer, page_table, w_uk, w_uv, g_k_mla)
    so_mem = memory_attention(s_qm, c_mem_k, c_mem_v)
    y_s = merge_and_ffn(x_s, so_mla, so_fox, so_mem, s_gates, *ffn_w)
    return (y_p, y_s, lat, kr, k_f, v_f, logf, mem_k, mem_v, s_lat, s_kr, s_kf, s_vf, s_logf)


def setup_inputs(seed: int = 0) -> dict:
    key = jax.random.key(seed)
    keys = list(jax.random.split(key, 64))

    def nrm(shape, scale=1.0):
        return scale * jax.random.normal(keys.pop(), shape, jnp.float32)

    def gain(n):
        return 1.0 + 0.02 * nrm((DEPTH, n))

    L = DEPTH
    n_pages = PAST_LEN // PAGE_SIZE
    n_pool = (DEC_BATCH * n_pages * 5) // 4
    page_table = jax.random.permutation(keys.pop(), n_pool)[:DEC_BATCH * n_pages].reshape(
        DEC_BATCH, n_pages).astype(jnp.int32)
    return {
        'x_prompt': nrm((BATCH, SEQ, D_MODEL)),
        'x_sample': nrm((DEC_BATCH, DEC_SEQ, D_MODEL)),
        'cache_mla_latent': nrm((L, n_pool, PAGE_SIZE, KV_LORA)),
        'cache_mla_krope': nrm((L, n_pool, PAGE_SIZE, ROPE_DIM)),
        'cache_fox_k': nrm((L, n_pool, PAGE_SIZE, FOX_HEADS, FOX_DIM)),
        'cache_fox_v': nrm((L, n_pool, PAGE_SIZE, FOX_HEADS, FOX_DIM)),
        'cache_fox_logf': jax.nn.log_sigmoid(2.0 + 0.5 * nrm((L, n_pool, PAGE_SIZE, FOX_HEADS))),
        'cache_mem_k': nrm((L, DEC_BATCH, MEM_LEN, MEM_HEADS, MEM_DIM)),
        'cache_mem_v': nrm((L, DEC_BATCH, MEM_LEN, MEM_HEADS, MEM_DIM)),
        'page_table': page_table,
        'mem_prompt': nrm((BATCH, MEM_LEN, D_MODEL)),
        'g_attn_norm': gain(D_MODEL),
        'w_in': nrm((L, D_MODEL, D_IN), D_MODEL ** -0.5),
        'b_f': 2.0 + 0.1 * nrm((L, FOX_HEADS)),
        'g_cq': gain(Q_LORA),
        'w_uq': nrm((L, Q_LORA, MLA_HEADS * QK_DIM_MLA), Q_LORA ** -0.5),
        'g_ckv': gain(KV_LORA),
        'w_uk': nrm((L, KV_LORA, MLA_HEADS, NOPE_DIM), KV_LORA ** -0.5),
        'w_uv': nrm((L, KV_LORA, MLA_HEADS, V_DIM_MLA), KV_LORA ** -0.5),
        'g_q_mla': gain(QK_DIM_MLA),
        'g_k_mla': gain(QK_DIM_MLA),
        'g_q_fox': gain(FOX_DIM),
        'g_k_fox': gain(FOX_DIM),
        'g_mem_norm': gain(D_MODEL),
        'w_mem_kv': nrm((L, D_MODEL, 2 * MEM_HEADS * MEM_DIM), D_MODEL ** -0.5),
        'g_q_mem': gain(MEM_DIM),
        'g_k_mem': gain(MEM_DIM),
        'w_br_mla': nrm((L, MLA_HEADS * V_DIM_MLA, D_MODEL), (MLA_HEADS * V_DIM_MLA) ** -0.5),
        'w_br_fox': nrm((L, FOX_HEADS * FOX_DIM, D_MODEL), (FOX_HEADS * FOX_DIM) ** -0.5),
        'w_br_mem': nrm((L, MEM_HEADS * MEM_DIM, D_MODEL), (MEM_HEADS * MEM_DIM) ** -0.5),
        'w_o': nrm((L, D_MODEL, D_MODEL), D_MODEL ** -0.5),
        'g_ffn_norm': gain(D_MODEL),
        'w_router_group': nrm((L, D_MODEL, N_GROUPS), D_MODEL ** -0.5),
        'b_router_group': nrm((L, N_GROUPS), 0.01),
        'w_router_expert': nrm((L, D_MODEL, N_EXPERTS), D_MODEL ** -0.5),
        'b_router_expert': nrm((L, N_EXPERTS), 0.01),
        'w_e_gate': nrm((L, N_EXPERTS, D_MODEL, D_FF_EXPERT), D_MODEL ** -0.5),
        'w_e_up': nrm((L, N_EXPERTS, D_MODEL, D_FF_EXPERT), D_MODEL ** -0.5),
        'w_e_down': nrm((L, N_EXPERTS, D_FF_EXPERT, D_MODEL), D_FF_EXPERT ** -0.5),
    }


def reference(x_prompt, x_sample, cache_mla_latent, cache_mla_krope, cache_fox_k, cache_fox_v,
              cache_fox_logf, cache_mem_k, cache_mem_v, page_table, mem_prompt,
              g_attn_norm, w_in, b_f, g_cq, w_uq, g_ckv, w_uk, w_uv, g_q_mla, g_k_mla,
              g_q_fox, g_k_fox, g_mem_norm, w_mem_kv, g_q_mem, g_k_mem,
              w_br_mla, w_br_fox, w_br_mem, w_o, g_ffn_norm,
              w_router_group, b_router_group, w_router_expert, b_router_expert,
              w_e_gate, w_e_up, w_e_down):
    h_p, h_s = x_prompt, x_sample
    new_rows = [[] for _ in range(12)]
    for l in range(DEPTH):
        out = decoder_layer(
            h_p, h_s, mem_prompt, cache_mla_latent, cache_mla_krope, cache_fox_k, cache_fox_v,
            cache_fox_logf, cache_mem_k[l], cache_mem_v[l], l, page_table,
            g_attn_norm[l], w_in[l], b_f[l], g_cq[l], w_uq[l], g_ckv[l], w_uk[l], w_uv[l],
            g_q_mla[l], g_k_mla[l], g_q_fox[l], g_k_fox[l], g_mem_norm[l], w_mem_kv[l],
            g_q_mem[l], g_k_mem[l], w_br_mla[l], w_br_fox[l], w_br_mem[l], w_o[l], g_ffn_norm[l],
            w_router_group[l], b_router_group[l], w_router_expert[l], b_router_expert[l],
            w_e_gate[l], w_e_up[l], w_e_down[l])
        h_p, h_s = out[0], out[1]
        for rows, r in zip(new_rows, out[2:]):
            rows.append(r)
    (p_mla_latent, p_mla_krope, p_fox_k, p_fox_v, p_fox_logf, p_mem_k, p_mem_v,
     s_mla_latent, s_mla_krope, s_fox_k, s_fox_v, s_fox_logf) = [jnp.stack(r) for r in new_rows]
    return (h_p, h_s, p_mla_latent, p_mla_krope, p_fox_k, p_fox_v, p_fox_logf, p_mem_k, p_mem_v,
            s_mla_latent, s_mla_krope, s_fox_k, s_fox_v, s_fox_logf)
```

```python
import functools

import jax
import jax.numpy as jnp
from jax import lax
from jax.experimental import pallas as pl
from jax.experimental.pallas import tpu as pltpu

_BF = jnp.bfloat16
_F32 = jnp.float32
_EPS = 1e-6
_NEG_INF = -1e30
_ROPE_THETA = 10000.0
_LANES = 128
_SUBLANES = 8
_VMEM_LIMIT = 56 * 1024 * 1024


def _dot(a, b):
    return jnp.dot(a, b, preferred_element_type=_F32)


def _dot_nt(a, b):
    return lax.dot_general(a, b, (((1,), (1,)), ((), ())), preferred_element_type=_F32)


def _split3(x):
    hi = x.astype(_BF)
    r1 = x - hi.astype(_F32)
    mid = r1.astype(_BF)
    lo = (r1 - mid.astype(_F32)).astype(_BF)
    return hi, mid, lo


def _dot01(m01, x):
    hi, mid, lo = _split3(x)
    return _dot(m01, hi) + _dot(m01, mid) + _dot(m01, lo)


def _dot_hi(a, b):
    ah = a.astype(_BF)
    al = (a - ah.astype(_F32)).astype(_BF)
    bh = b.astype(_BF)
    bl = (b - bh.astype(_F32)).astype(_BF)
    return _dot(ah, bh) + _dot(al, bh) + _dot(ah, bl)


def _rms(x, g, n=None):
    n = x.shape[-1] if n is None else n
    ms = jnp.sum(x * x, axis=-1, keepdims=True) * (1.0 / n)
    return x * lax.rsqrt(ms + _EPS) * g


def _log_sigmoid(x):
    return jnp.minimum(x, 0.0) - jnp.log1p(jnp.exp(-jnp.abs(x)))


def _sigmoid(x):
    return 1.0 / (1.0 + jnp.exp(-x))


def _lane_prefix(x):
    lane = lax.broadcasted_iota(jnp.int32, x.shape, 1)
    s = 1
    while s < _LANES:
        x = x + jnp.where(lane >= s, pltpu.roll(x, s, 1), 0.0)
        s *= 2
    return x


def _lane_suffix(x):
    lane = lax.broadcasted_iota(jnp.int32, x.shape, 1)
    s = 1
    while s < _LANES:
        x = x + jnp.where(lane < _LANES - s, pltpu.roll(x, _LANES - s, 1), 0.0)
        s *= 2
    return x


def _tile(n, pref, mult=_SUBLANES):
    t = min(pref, n) // mult * mult
    while n % t:
        t -= mult
    return t


def _const(shape):
    nd = len(shape)
    return pl.BlockSpec(shape, lambda *_: (0,) * nd, pipeline_mode=pl.Buffered(1))


def _params(sem):
    return pltpu.CompilerParams(dimension_semantics=sem, vmem_limit_bytes=_VMEM_LIMIT)


def _proj_body(x_ref, gat_ref, wsm_ref, wflt_ref, wflr_ref, bfc_ref, bfr_ref, gcq_ref, gckv_ref,
               wq_ref, wqr_ref, rc_ref, rs_ref, gqm_ref, gqf_ref, gkf_ref, gqe_ref,
               wuk_ref, wuv_ref, gkn_ref, gkr_ref, tri_ref,
               qmla_ref, lat_ref, krope_ref, qfox_ref, kfox_ref, vfox_ref, kfoxb_ref, vfoxb_ref,
               lft_ref, ft_ref, lfr_ref, fr_ref, qmem_ref, kmla_ref, vmla_ref, xn_ref,
               ct_ref, cr_ref, *, dims):
    ql, kvl, rd, nope, hm, fh, fd, mh, md, hp = dims
    tm = x_ref.shape[0]

    @pl.when(pl.program_id(0) == 0)
    def _():
        ct_ref[...] = jnp.zeros_like(ct_ref)
        cr_ref[...] = jnp.zeros_like(cr_ref)

    xb = _rms(x_ref[...], gat_ref[...]).astype(_BF)
    xn_ref[...] = xb
    proj = _dot(xb, wsm_ref[...])
    o = 0
    cq = proj[:, o:o + ql]; o += ql
    ckv = proj[:, o:o + kvl]; o += kvl
    krb = proj[:, o:o + _LANES]; o += _LANES
    krr = proj[:, o:o + _LANES]; o += _LANES
    fq = proj[:, o:o + fh * fd]; o += fh * fd
    fk = proj[:, o:o + fh * fd]; o += fh * fd
    fv = proj[:, o:o + fh * fd]; o += fh * fd
    mq = proj[:, o:o + mh * md]

    rc = rc_ref[...]
    rs = rs_ref[...]
    qk_dim = nope + rd

    cqn = _rms(cq, gcq_ref[...]).astype(_BF)
    q = _dot(cqn, wq_ref[...])
    qr = _dot(cqn, wqr_ref[...])
    for h in range(hm):
        sl = slice(h * hp, (h + 1) * hp)
        qh = q[:, sl] * rc + qr[:, sl] * rs
        qmla_ref[:, sl] = _rms(qh, gqm_ref[...], qk_dim).astype(_BF)

    lat = _rms(ckv, gckv_ref[...])
    lat_ref[...] = lat
    kr = krb * rc[:, nope:nope + _LANES] + krr * rs[:, nope:nope + _LANES]
    krope_ref[...] = kr[:, 0:rd]

    latb = lat.astype(_BF)
    kn = _dot(latb, wuk_ref[...])
    vmla_ref[...] = _dot(latb, wuv_ref[...]).astype(_BF)
    ssr = jnp.sum(kr * kr, axis=-1, keepdims=True)
    for h in range(hm):
        knh = kn[:, h * nope:(h + 1) * nope]
        rinv = lax.rsqrt((jnp.sum(knh * knh, axis=-1, keepdims=True) + ssr) * (1.0 / qk_dim) + _EPS)
        kmla_ref[:, h * hp:h * hp + nope] = (knh * rinv * gkn_ref[...]).astype(_BF)
        kmla_ref[:, h * hp + nope:(h + 1) * hp] = (kr * rinv * gkr_ref[...]).astype(_BF)

    for h in range(fh):
        sl = slice(h * fd, (h + 1) * fd)
        qfox_ref[:, sl] = _rms(fq[:, sl], gqf_ref[...]).astype(_BF)
        kf = _rms(fk[:, sl], gkf_ref[...])
        kfox_ref[:, sl] = kf
        kfoxb_ref[:, sl] = kf.astype(_BF)
    vfox_ref[...] = fv
    vfoxb_ref[...] = fv.astype(_BF)
    for h in range(mh):
        sl = slice(h * md, (h + 1) * md)
        qmem_ref[:, sl] = _rms(mq[:, sl], gqe_ref[...]).astype(_BF)

    row = lax.broadcasted_iota(jnp.int32, (_SUBLANES, tm), 0)
    lft = jnp.where(row < fh, _log_sigmoid(_dot_nt(wflt_ref[...], xb) + bfc_ref[:, 0:1]), 0.0)
    lft_ref[...] = lft
    carry = ct_ref[:, 0:1]
    for j in range(tm // _LANES):
        c = _lane_prefix(lft[:, j * _LANES:(j + 1) * _LANES]) + carry
        ft_ref[:, j * _LANES:(j + 1) * _LANES] = c
        carry = c[:, _LANES - 1:_LANES]
    ct_ref[...] = jnp.broadcast_to(carry, ct_ref.shape)

    lane = lax.broadcasted_iota(jnp.int32, (tm, _LANES), 1)
    lfr = jnp.where(lane < fh, _log_sigmoid(_dot(xb, wflr_ref[...]) + bfr_ref[...]), 0.0)
    lfr_ref[...] = lfr
    fr = _dot01(tri_ref[...], lfr) + cr_ref[0:1, :]
    fr_ref[...] = fr
    cr_ref[...] = jnp.broadcast_to(fr[tm - 1:tm, :], cr_ref.shape)


def _gate_body(xn_ref, wg_ref, o_ref):
    o_ref[...] = _sigmoid(_dot(xn_ref[...], wg_ref[...]))


def _flash_body(*refs, heads, dk, dv, scale, bias):
    if bias:
        q_ref, k_ref, v_ref, fr_ref, ft_ref, o_ref, m_ref, l_ref, acc_ref = refs
    else:
        q_ref, k_ref, v_ref, o_ref, m_ref, l_ref, acc_ref = refs
    qi = pl.program_id(0)
    ki = pl.program_id(1)
    tq = q_ref.shape[0]
    tk = k_ref.shape[0]

    @pl.when(ki == 0)
    def _():
        m_ref[...] = jnp.full_like(m_ref, _NEG_INF)
        l_ref[...] = jnp.zeros_like(l_ref)
        acc_ref[...] = jnp.zeros_like(acc_ref)

    def step(masked):
        if masked:
            keep = (lax.broadcasted_iota(jnp.int32, (tq, tk), 0) >= lax.broadcasted_iota(jnp.int32, (tq, tk), 1))
        for h in range(heads):
            s = _dot_nt(q_ref[:, h * dk:(h + 1) * dk], k_ref[:, h * dk:(h + 1) * dk]) * scale
            if bias:
                s = s + fr_ref[:, h:h + 1] - ft_ref[h:h + 1, :]
            if masked:
                s = jnp.where(keep, s, _NEG_INF)
            m_prev = m_ref[h]
            m_new = jnp.maximum(m_prev, jnp.max(s, axis=-1, keepdims=True))
            alpha = jnp.exp(m_prev - m_new)
            p = jnp.exp(s - m_new)
            l_ref[h] = alpha * l_ref[h] + jnp.sum(p, axis=-1, keepdims=True)
            acc_ref[h] = alpha * acc_ref[h] + _dot(p.astype(_BF), v_ref[:, h * dv:(h + 1) * dv])
            m_ref[h] = m_new

    @pl.when(ki < qi)
    def _():
        step(False)

    @pl.when(ki == qi)
    def _():
        step(True)
        for h in range(heads):
            o_ref[:, h * dv:(h + 1) * dv] = (acc_ref[h] / l_ref[h]).astype(o_ref.dtype)


def _flash(q, k, v, fr, ft, t, heads, dk, dv, scale, tq):
    nq = t // tq
    bias = fr is not None
    kmap = lambda i, j: (jnp.minimum(i, j), 0)
    in_specs = [pl.BlockSpec((tq, heads * dk), lambda i, j: (i, 0)),
                pl.BlockSpec((tq, heads * dk), kmap),
                pl.BlockSpec((tq, heads * dv), kmap)]
    args = [q, k, v]
    if bias:
        in_specs += [pl.BlockSpec((tq, _LANES), lambda i, j: (i, 0)),
                     pl.BlockSpec((_SUBLANES, tq), lambda i, j: (0, jnp.minimum(i, j)))]
        args += [fr, ft]
    return pl.pallas_call(
        functools.partial(_flash_body, heads=heads, dk=dk, dv=dv, scale=scale, bias=bias),
        grid=(nq, nq),
        in_specs=in_specs,
        out_specs=pl.BlockSpec((tq, heads * dv), lambda i, j: (i, 0)),
        out_shape=jax.ShapeDtypeStruct((t, heads * dv), _BF),
        scratch_shapes=[pltpu.VMEM((heads, tq, 1), _F32), pltpu.VMEM((heads, tq, 1), _F32),
                        pltpu.VMEM((heads, tq, dv), _F32)],
        compiler_params=_params(("parallel", "arbitrary")),
        name="flash_bias" if bias else "flash_mla",
    )(*args)


def _memkv_body(mem_ref, g_ref, w_ref, gk_ref, k_ref, v_ref, kb_ref, vb_ref, *, heads, d):
    mn = _rms(mem_ref[...], g_ref[...]).astype(_BF)
    kv = _dot(mn, w_ref[...])
    for h in range(heads):
        sl = slice(h * d, (h + 1) * d)
        k = _rms(kv[:, sl], gk_ref[...])
        k_ref[:, sl] = k
        kb_ref[:, sl] = k.astype(_BF)
    v = kv[:, heads * d:]
    v_ref[...] = v
    vb_ref[...] = v.astype(_BF)


def _memattn_body(q_ref, k_ref, v_ref, o_ref, *, heads, d, scale):
    for h in range(heads):
        sl = slice(h * d, (h + 1) * d)
        s = _dot_nt(q_ref[:, sl], k_ref[:, sl]) * scale
        p = jnp.exp(s - jnp.max(s, axis=-1, keepdims=True))
        o = _dot(p.astype(_BF), v_ref[:, sl]) / jnp.sum(p, axis=-1, keepdims=True)
        o_ref[:, sl] = o.astype(o_ref.dtype)


def _sprep_body(q_ref, gkn_ref, gkr_ref, wukh_ref, lfr_ref, g4_ref, qhi_ref, qlo_ref, qr_ref, c_ref, *, hm, nope, rd, hp):
    for h in range(hm):
        qn = (q_ref[:, h * hp:h * hp + nope].astype(_F32) * gkn_ref[...]).astype(_BF)
        ql = _dot(qn, wukh_ref[h])
        hi = ql.astype(_BF)
        qhi_ref[h] = hi
        qlo_ref[h] = (ql - hi.astype(_F32)).astype(_BF)
        qr_ref[h] = (q_ref[:, h * hp + nope:h * hp + nope + _LANES].astype(_F32) * gkr_ref[...]).astype(_BF)
    c_ref[...] = _dot01(g4_ref[...], lfr_ref[...])


def _sample_body(pt_ref, *refs, cp, hm, fh, ds, kvl, rd, nope, mla_scale, fox_scale):
    lat_refs = refs[0:cp]
    kr_refs = refs[cp:2 * cp]
    fk_refs = refs[2 * cp:3 * cp]
    fv_refs = refs[3 * cp:4 * cp]
    lf_refs = refs[4 * cp:5 * cp]
    (qlat_ref, qr_ref, qf_ref, crow_ref, ckey_ref, latn_ref, krn_ref, fkn_ref, fvn_ref,
     wukt_ref, wuv_ref, hmask_ref, omla_ref, ofox_ref,
     mm_ref, lm_ref, am_ref, mf_ref, lf_ref, af_ref, car_ref, latb_ref) = refs[5 * cp:]
    c = pl.program_id(1)
    nc = pl.num_programs(1)
    ps = _LANES
    rows_m = ds * _SUBLANES
    rows_f = fh * _SUBLANES
    qk_dim = nope + rd
    rid8 = lax.broadcasted_iota(jnp.int32, (_SUBLANES, 1), 0)

    def mla_scores(latb, krt):
        n = latb.shape[0]
        kn = _dot_nt(wukt_ref[...], latb)
        ssq = jnp.zeros((_SUBLANES, n), _F32)
        for h in range(hm):
            knh = kn[h * nope:(h + 1) * nope]
            ssq = ssq + jnp.where(rid8 == h, jnp.sum(knh * knh, axis=0, keepdims=True), 0.0)
        ssr = jnp.sum(krt * krt, axis=0, keepdims=True)
        rinv = lax.rsqrt((ssq + ssr) * (1.0 / qk_dim) + _EPS)
        sl = _dot_nt(qlat_ref[...], latb)
        s = sl[0:rows_m] + sl[rows_m:2 * rows_m] + _dot(qr_ref[...], krt.astype(_BF))
        return s * jnp.concatenate([rinv] * ds, axis=0) * mla_scale

    def fox_scores(k_list, bias_list):
        out = []
        for h in range(fh):
            kh = jnp.concatenate([r[pl.ds(h, ps, stride=fh), :] for r in k_list], axis=0).astype(_BF)
            bh = jnp.concatenate([b[h:h + 1, :] for b in bias_list], axis=1)
            qh = qf_ref[h * _SUBLANES:(h + 1) * _SUBLANES, :]
            out.append(_dot_nt(qh, kh) * fox_scale + crow_ref[h * _SUBLANES:(h + 1) * _SUBLANES, 0:1] + bh)
        return jnp.concatenate(out, axis=0)

    def softmax_update(s, m_ref, l_ref):
        m_prev = m_ref[:, 0:1]
        m_new = jnp.maximum(m_prev, jnp.max(s, axis=-1, keepdims=True))
        alpha = jnp.exp(m_prev - m_new)
        p = jnp.exp(s - m_new)
        l_ref[...] = jnp.broadcast_to(alpha * l_ref[:, 0:1] + jnp.sum(p, axis=-1, keepdims=True), l_ref.shape)
        m_ref[...] = jnp.broadcast_to(m_new, m_ref.shape)
        return alpha, p

    def fox_pv(p, v_list, off):
        n = len(v_list) * ps
        for h in range(fh):
            vh = jnp.concatenate([r[pl.ds(h, ps, stride=fh), :] for r in v_list], axis=0).astype(_BF)
            ph = p[h * _SUBLANES:(h + 1) * _SUBLANES, off:off + n].astype(_BF)
            af_ref[h * _SUBLANES:(h + 1) * _SUBLANES, :] += _dot(ph, vh)

    @pl.when(c == 0)
    def _():
        mm_ref[...] = jnp.full_like(mm_ref, _NEG_INF)
        lm_ref[...] = jnp.zeros_like(lm_ref)
        am_ref[...] = jnp.zeros_like(am_ref)
        mf_ref[...] = jnp.full_like(mf_ref, _NEG_INF)
        lf_ref[...] = jnp.zeros_like(lf_ref)
        af_ref[...] = jnp.zeros_like(af_ref)
        car_ref[...] = jnp.zeros_like(car_ref)
        latb = latn_ref[...].astype(_BF)
        key = lax.broadcasted_iota(jnp.int32, (rows_m, ps), 1)
        tok = lax.broadcasted_iota(jnp.int32, (rows_m, ps), 0) >> 3
        s = jnp.where((key < ds) & (key <= tok), mla_scores(latb, krn_ref[...]), _NEG_INF)
        alpha, p = softmax_update(s, mm_ref, lm_ref)
        am_ref[...] = alpha * am_ref[...] + _dot(p.astype(_BF), latb)
        key = lax.broadcasted_iota(jnp.int32, (rows_f, ps), 1)
        tok = lax.broadcasted_iota(jnp.int32, (rows_f, ps), 0) & (_SUBLANES - 1)
        sf = jnp.where((key < ds) & (key <= tok), fox_scores([fkn_ref], [-ckey_ref[...]]), _NEG_INF)
        alpha, p = softmax_update(sf, mf_ref, lf_ref)
        af_ref[...] = alpha * af_ref[...]
        fox_pv(p, [fvn_ref], 0)

    carry = car_ref[0:fh, :]
    after = [None] * cp
    for g in reversed(range(cp)):
        x = lf_refs[g][...]
        suf = _lane_suffix(x)
        after[g] = carry + suf - x
        carry = carry + suf[:, 0:1]
    car_ref[0:fh, :] = carry

    sm = []
    sf = []
    for g in range(0, cp, 2):
        latb = jnp.concatenate([lat_refs[g][...], lat_refs[g + 1][...]], axis=0).astype(_BF)
        latb_ref[g * ps:(g + 2) * ps, :] = latb
        krt = jnp.concatenate([kr_refs[g][...], kr_refs[g + 1][...]], axis=1)
        sm.append(mla_scores(latb, krt))
        sf.append(fox_scores(fk_refs[g:g + 2], after[g:g + 2]))
    alpha, p = softmax_update(jnp.concatenate(sm, axis=1), mm_ref, lm_ref)
    acc = alpha * am_ref[...]
    for g in range(0, cp, 2):
        acc = acc + _dot(p[:, g * ps:(g + 2) * ps].astype(_BF), latb_ref[g * ps:(g + 2) * ps, :])
    am_ref[...] = acc
    alpha, p = softmax_update(jnp.concatenate(sf, axis=1), mf_ref, lf_ref)
    af_ref[...] = alpha * af_ref[...]
    for g in range(0, cp, 2):
        fox_pv(p, fv_refs[g:g + 2], g * ps)

    @pl.when(c == nc - 1)
    def _():
        accn = (am_ref[...] / lm_ref[:, 0:1]).astype(_BF)
        o = _dot(accn, wuv_ref[...])
        for t in range(ds):
            blk = jnp.where(hmask_ref[...] > 0, o[t * _SUBLANES:(t + 1) * _SUBLANES], 0.0)
            omla_ref[t:t + 1, :] = jnp.sum(blk, axis=0, keepdims=True)
        ofox_ref[...] = af_ref[...] / lf_ref[:, 0:1]


def _smem_body(q_ref, k_ref, v_ref, o_ref, *, mh, keys, scale):
    for h in range(mh):
        kh = k_ref[pl.ds(h, keys, stride=mh), :].astype(_BF)
        vh = v_ref[pl.ds(h, keys, stride=mh), :].astype(_BF)
        s = _dot_nt(q_ref[h * _SUBLANES:(h + 1) * _SUBLANES, :], kh) * scale
        p = jnp.exp(s - jnp.max(s, axis=-1, keepdims=True))
        o_ref[h * _SUBLANES:(h + 1) * _SUBLANES, :] = _dot(p.astype(_BF), vh) / jnp.sum(p, axis=-1, keepdims=True)


def _merge_body(x_ref, g_ref, om_ref, of_ref, oe_ref, wbm_ref, wbf_ref, wbe_ref, wo_ref, gffn_ref, wr_ref, br_ref,
                h_ref, hn_ref, route_ref, *, d, ng, epg):
    g = g_ref[...]
    merged = (g[:, 0:d] * _dot(om_ref[...], wbm_ref[...]) + g[:, d:2 * d] * _dot(of_ref[...], wbf_ref[...])
              + g[:, 2 * d:3 * d] * _dot(oe_ref[...], wbe_ref[...]))
    h = x_ref[...] + _dot(merged.astype(_BF), wo_ref[...])
    h_ref[...] = h
    hn = _rms(h, gffn_ref[...])
    hn_ref[...] = hn
    logits = _dot_hi(hn, wr_ref[...]) + br_ref[...]
    tm = logits.shape[0]
    lane = lax.broadcasted_iota(jnp.int32, (tm, _LANES), 1).astype(_F32)
    big = float(4 * _LANES)
    isg = lane < ng
    lg = jnp.where(isg, logits, _NEG_INF)
    eg = jnp.where(isg, jnp.exp(lg - jnp.max(lg, axis=-1, keepdims=True)), 0.0)
    pg = eg / jnp.sum(eg, axis=-1, keepdims=True)
    pg_max = jnp.max(pg, axis=-1, keepdims=True)
    gidx = jnp.min(jnp.where(isg & (pg == pg_max), lane, big), axis=-1, keepdims=True)
    lo = ng + gidx * epg
    ine = (lane >= lo) & (lane < lo + epg)
    le = jnp.where(ine, logits, _NEG_INF)
    ee = jnp.where(ine, jnp.exp(le - jnp.max(le, axis=-1, keepdims=True)), 0.0)
    pe = ee / jnp.sum(ee, axis=-1, keepdims=True)
    p1 = jnp.max(jnp.where(ine, pe, -1.0), axis=-1, keepdims=True)
    i1 = jnp.min(jnp.where(ine & (pe == p1), lane, big), axis=-1, keepdims=True)
    rest = ine & (lane != i1)
    p2 = jnp.max(jnp.where(rest, pe, -1.0), axis=-1, keepdims=True)
    i2 = jnp.min(jnp.where(rest & (pe == p2), lane, big), axis=-1, keepdims=True)
    den = p1 + p2
    w1 = pg_max * p1 / den
    w2 = pg_max * p2 / den
    route_ref[...] = jnp.where(lane == 0, i1 - ng, jnp.where(lane == 1, i2 - ng,
                                                             jnp.where(lane == 2, w1, jnp.where(lane == 3, w2, 0.0))))


def _moe_body(te_ref, rt_ref, na_ref, hn_hbm, wg_ref, wu_ref, wd_ref, y_ref, xbuf, sem):
    i = pl.program_id(0)
    tm = xbuf.shape[0]

    def row_copy(r, tok):
        return pltpu.make_async_copy(hn_hbm.at[pl.ds(tok, 1)], xbuf.at[pl.ds(r, 1)], sem)

    @pl.when(i < na_ref[0])
    def _():
        def issue(r, carry):
            row_copy(r, rt_ref[i * tm + r]).start()
            return carry
        lax.fori_loop(0, tm, issue, 0)

        def drain(r, carry):
            row_copy(r, 0).wait()
            return carry
        lax.fori_loop(0, tm, drain, 0)
        xb = xbuf[...].astype(_BF)
        gate = _dot(xb, wg_ref[...].astype(_BF))
        up = _dot(xb, wu_ref[...].astype(_BF))
        hmid = gate * _sigmoid(gate) * up
        y_ref[...] = _dot(hmid.astype(_BF), wd_ref[...].astype(_BF))

    @pl.when(i >= na_ref[0])
    def _():
        y_ref[...] = jnp.zeros_like(y_ref)


def _combine_body(pos_ref, y_hbm, h_ref, route_ref, o_ref, ybuf, sem):
    i = pl.program_id(0)
    tm = h_ref.shape[0]

    def row_copy(k, r, src):
        return pltpu.make_async_copy(y_hbm.at[pl.ds(src, 1)], ybuf.at[k, pl.ds(r, 1)], sem)

    def issue(r, carry):
        row_copy(0, r, pos_ref[2 * (i * tm + r)]).start()
        row_copy(1, r, pos_ref[2 * (i * tm + r) + 1]).start()
        return carry
    lax.fori_loop(0, tm, issue, 0)

    def drain(r, carry):
        row_copy(0, r, 0).wait()
        row_copy(1, r, 0).wait()
        return carry
    lax.fori_loop(0, tm, drain, 0)
    rt = route_ref[...]
    o_ref[...] = h_ref[...] + (rt[:, 2:3] * ybuf[0] + rt[:, 3:4] * ybuf[1])


def _layer(l, x_all, t, caches, page_table, mem_prompt, w):
    (cache_lat, cache_kr, cache_fk, cache_fv, cache_lf, cache_mk, cache_mv) = caches
    n, d = x_all.shape
    db, npg = page_table.shape
    ds = (n - t) // db
    ps = cache_lat.shape[2]
    kvl = cache_lat.shape[3]
    rd = cache_kr.shape[3]
    fh, fd = cache_fk.shape[3], cache_fk.shape[4]
    mh, md = cache_mk.shape[3], cache_mk.shape[4]
    mlen = cache_mk.shape[2]
    hm, nope = w["w_uk"].shape[2], w["w_uk"].shape[3]
    dv = w["w_uv"].shape[3]
    ql = w["w_uq"].shape[1]
    qk_dim = nope + rd
    hp = -(-qk_dim // _LANES) * _LANES
    ng = w["w_router_group"].shape[2]
    ne = w["w_router_expert"].shape[2]
    epg = ne // ng
    dff = w["w_e_gate"].shape[3]
    assert ps == _LANES and nope == _LANES and fd == _LANES and md == _LANES and dv == _LANES
    assert 2 * rd == _LANES and hp == 2 * _LANES and hm == _SUBLANES and ds <= _SUBLANES
    assert ng + ne <= _LANES and fh <= _SUBLANES and mh == fh

    mla_scale = float(qk_dim) ** -0.5
    fox_scale = float(fd) ** -0.5
    mem_scale = float(md) ** -0.5

    w_in = w["w_in"][l]
    widths = (ql, kvl, rd, fh * fd, fh * fd, fh * fd, fh, mh * md, 3 * d)
    offs = [0]
    for wd_ in widths:
        offs.append(offs[-1] + wd_)
    seg = lambda i: w_in[:, offs[i]:offs[i + 1]]
    half = rd // 2
    rot = jnp.concatenate([jnp.arange(half, rd), jnp.arange(0, half)])
    zpad = jnp.zeros((d, _LANES - rd), _F32)
    wsm = jnp.concatenate([seg(0), seg(1), seg(2), zpad, seg(2)[:, rot], zpad, seg(3), seg(4), seg(5), seg(7)],
                          axis=1).astype(_BF)
    wfl = seg(6)
    wflt = jnp.zeros((_SUBLANES, d), _F32).at[0:fh].set(wfl.T).astype(_BF)
    wflr = jnp.zeros((d, _LANES), _F32).at[:, 0:fh].set(wfl).astype(_BF)
    bfc = jnp.zeros((_SUBLANES, _LANES), _F32).at[0:fh, :].set(jnp.broadcast_to(w["b_f"][l][:, None], (fh, _LANES)))
    bfr = jnp.zeros((1, _LANES), _F32).at[0, 0:fh].set(w["b_f"][l])
    wgate = seg(8).astype(_BF)
    w_uq = w["w_uq"][l].reshape(ql, hm, qk_dim)
    zq = jnp.zeros((ql, hm, hp - qk_dim), _F32)
    zn = jnp.zeros((ql, hm, nope), _F32)
    wq = jnp.concatenate([w_uq, zq], axis=2).reshape(ql, hm * hp).astype(_BF)
    wqr = jnp.concatenate([zn, w_uq[:, :, nope:][:, :, rot], zq], axis=2).reshape(ql, hm * hp).astype(_BF)
    gqm = jnp.zeros((1, hp), _F32).at[0, 0:qk_dim].set(w["g_q_mla"][l])
    gkn = w["g_k_mla"][l][None, 0:nope]
    gkr = jnp.zeros((1, _LANES), _F32).at[0, 0:rd].set(w["g_k_mla"][l][nope:])
    w_uk = w["w_uk"][l]
    wuk_flat = w_uk.reshape(kvl, hm * nope).astype(_BF)
    wukt = wuk_flat.T
    wukh = jnp.transpose(w_uk, (1, 2, 0)).astype(_BF)
    wuv_flat = w["w_uv"][l].reshape(kvl, hm * dv).astype(_BF)
    row2 = lambda v: v[None, :]

    pos = jnp.concatenate([jnp.arange(t, dtype=_F32), jnp.tile(npg * ps + jnp.arange(ds, dtype=_F32), db)])
    inv = _ROPE_THETA ** (-jnp.arange(half, dtype=_F32) / half)
    ang = pos[:, None] * inv[None, :]
    cos, sin = jnp.cos(ang), jnp.sin(ang)
    zt = jnp.zeros((n, hp - qk_dim), _F32)
    rope_c = jnp.concatenate([jnp.ones((n, nope), _F32), cos, cos, zt], axis=1)
    rope_s = jnp.concatenate([jnp.zeros((n, nope), _F32), -sin, sin, zt], axis=1)

    tm = _tile(n, 256, _LANES)
    tri = jnp.tri(tm, dtype=_BF)
    dims = (ql, kvl, rd, nope, hm, fh, fd, mh, md, hp)
    tok = lambda width: pl.BlockSpec((tm, width), lambda i: (i, 0))
    lan = pl.BlockSpec((_SUBLANES, tm), lambda i: (0, i))
    sds = jax.ShapeDtypeStruct
    proj_out = pl.pallas_call(
        functools.partial(_proj_body, dims=dims),
        grid=(n // tm,),
        in_specs=[tok(d), _const((1, d)), _const(wsm.shape), _const(wflt.shape), _const(wflr.shape),
                  _const(bfc.shape), _const(bfr.shape), _const((1, ql)), _const((1, kvl)),
                  _const(wq.shape), _const(wqr.shape), tok(hp), tok(hp), _const((1, hp)),
                  _const((1, fd)), _const((1, fd)), _const((1, md)),
                  _const(wuk_flat.shape), _const(wuv_flat.shape), _const((1, nope)), _const((1, _LANES)),
                  _const(tri.shape)],
        out_specs=[tok(hm * hp), tok(kvl), tok(rd), tok(fh * fd), tok(fh * fd), tok(fh * fd), tok(fh * fd),
                   tok(fh * fd), lan, lan, tok(_LANES), tok(_LANES), tok(mh * md), tok(hm * hp), tok(hm * dv), tok(d)],
        out_shape=[sds((n, hm * hp), _BF), sds((n, kvl), _F32), sds((n, rd), _F32), sds((n, fh * fd), _BF),
                   sds((n, fh * fd), _F32), sds((n, fh * fd), _F32), sds((n, fh * fd), _BF), sds((n, fh * fd), _BF),
                   sds((_SUBLANES, n), _F32), sds((_SUBLANES, n), _F32), sds((n, _LANES), _F32), sds((n, _LANES), _F32),
                   sds((n, mh * md), _BF), sds((n, hm * hp), _BF), sds((n, hm * dv), _BF), sds((n, d), _BF)],
        scratch_shapes=[pltpu.VMEM((_SUBLANES, _LANES), _F32), pltpu.VMEM((_SUBLANES, _LANES), _F32)],
        compiler_params=_params(("arbitrary",)),
        name="token_proj",
    )(x_all, row2(w["g_attn_norm"][l]), wsm, wflt, wflr, bfc, bfr, row2(w["g_cq"][l]), row2(w["g_ckv"][l]),
      wq, wqr, rope_c, rope_s, gqm, row2(w["g_q_fox"][l]), row2(w["g_k_fox"][l]), row2(w["g_q_mem"][l]),
      wuk_flat, wuv_flat, gkn, gkr, tri)
    (qmla, lat, krope, qfox, kfox, vfox, kfoxb, vfoxb, lft, ft, lfr, fr, qmem, kmla, vmla, xnb) = proj_out

    tg = _tile(n, 512)
    tn = d
    gates = pl.pallas_call(
        _gate_body,
        grid=(3 * d // tn, n // tg),
        in_specs=[pl.BlockSpec((tg, d), lambda j, i: (i, 0)), pl.BlockSpec((d, tn), lambda j, i: (0, j))],
        out_specs=pl.BlockSpec((tg, tn), lambda j, i: (i, j)),
        out_shape=sds((n, 3 * d), _F32),
        compiler_params=_params(("parallel", "parallel")),
        name="branch_gates",
    )(xnb, wgate)

    tq = _tile(t, 512)
    o_mla_p = _flash(qmla, kmla, vmla, None, None, t, hm, hp, dv, mla_scale, tq)
    o_fox_p = _flash(qfox, kfoxb, vfoxb, fr, ft, t, fh, fd, fd, fox_scale, tq)

    wmkv = w["w_mem_kv"][l].astype(_BF)
    mem_k, mem_v, mem_kb, mem_vb = pl.pallas_call(
        functools.partial(_memkv_body, heads=mh, d=md),
        out_shape=[sds((mlen, mh * md), _F32), sds((mlen, mh * md), _F32), sds((mlen, mh * md), _BF), sds((mlen, mh * md), _BF)],
        compiler_params=pltpu.CompilerParams(vmem_limit_bytes=_VMEM_LIMIT),
        name="memory_kv",
    )(mem_prompt, row2(w["g_mem_norm"][l]), wmkv, row2(w["g_k_mem"][l]))
    o_mem_p = pl.pallas_call(
        functools.partial(_memattn_body, heads=mh, d=md, scale=mem_scale),
        grid=(t // tq,),
        in_specs=[pl.BlockSpec((tq, mh * md), lambda i: (i, 0)), _const((mlen, mh * md)), _const((mlen, mh * md))],
        out_specs=pl.BlockSpec((tq, mh * md), lambda i: (i, 0)),
        out_shape=sds((t, mh * md), _BF),
        compiler_params=_params(("parallel",)),
        name="memory_attn_prompt",
    )(qmem, mem_kb, mem_vb)

    ns = n - t
    g4 = (jnp.tri(ns, dtype=_F32) * (jnp.arange(ns)[:, None] // ds == jnp.arange(ns)[None, :] // ds)).astype(_BF)
    qhi, qlo, qrs, cnew = pl.pallas_call(
        functools.partial(_sprep_body, hm=hm, nope=nope, rd=rd, hp=hp),
        out_shape=[sds((hm, ns, kvl), _BF), sds((hm, ns, kvl), _BF), sds((hm, ns, _LANES), _BF), sds((ns, _LANES), _F32)],
        compiler_params=pltpu.CompilerParams(vmem_limit_bytes=_VMEM_LIMIT),
        name="sample_prep",
    )(qmla[t:], gkn, gkr, wukh, lfr[t:], g4)
    to_rows = lambda a: jnp.transpose(a.reshape(hm, db, ds, a.shape[-1]), (1, 2, 0, 3)).reshape(db, ds * hm, a.shape[-1])
    qlat = jnp.concatenate([to_rows(qhi), to_rows(qlo)], axis=1)
    qrr = to_rows(qrs)[:, :, 0:rd]

    def head_rows(a):
        a = jnp.transpose(a.reshape(db, ds, fh, fd), (0, 2, 1, 3))
        return jnp.pad(a, ((0, 0), (0, 0), (0, _SUBLANES - ds), (0, 0))).reshape(db, fh * _SUBLANES, fd)
    qf_rows = head_rows(qfox[t:])
    qe_rows = head_rows(qmem[t:])
    c3 = cnew[:, 0:fh].reshape(db, ds, fh)
    crow = jnp.pad(jnp.transpose(c3, (0, 2, 1)), ((0, 0), (0, 0), (0, _SUBLANES - ds))).reshape(db, fh * _SUBLANES, 1)
    crow = jnp.broadcast_to(crow, (db, fh * _SUBLANES, _LANES))
    ckey = jnp.pad(jnp.transpose(c3, (0, 2, 1)), ((0, 0), (0, 0), (0, ps - ds)))
    pad_keys = lambda a: jnp.pad(a.reshape(db, ds, -1), ((0, 0), (0, ps - ds), (0, 0)))
    lat_new = pad_keys(lat[t:])
    kr_new = jnp.transpose(pad_keys(krope[t:]), (0, 2, 1))
    fk_new = pad_keys(kfox[t:]).reshape(db, ps * fh, fd)
    fv_new = pad_keys(vfox[t:]).reshape(db, ps * fh, fd)
    hmask = (jnp.arange(hm * dv)[None, :] // dv == jnp.arange(hm)[:, None]).astype(_F32)

    n_pool = cache_lat.shape[1]
    kr_t = jnp.swapaxes(cache_kr, 2, 3)
    lf_t = jnp.swapaxes(cache_lf, 2, 3)
    fk4 = cache_fk.reshape(cache_fk.shape[0], n_pool, ps * fh, fd)
    fv4 = cache_fv.reshape(cache_fv.shape[0], n_pool, ps * fh, fd)
    cp = 8 if npg % 8 == 0 else (2 if npg % 2 == 0 else 1)
    assert cp % 2 == 0
    nc = npg // cp
    pt = page_table.reshape(-1)

    def page_spec(shape, g):
        def imap(b, c, pt_ref):
            return (l, pt_ref[b * npg + (nc - 1 - c) * cp + g], 0, 0)
        return pl.BlockSpec((None, None) + shape, imap)
    seq = lambda shape: pl.BlockSpec((None,) + shape, lambda b, c, pt_ref: (b,) + (0,) * len(shape))
    cst = lambda shape: pl.BlockSpec(shape, lambda b, c, pt_ref: (0,) * len(shape))
    in_specs = ([page_spec((ps, kvl), g) for g in range(cp)] + [page_spec((rd, ps), g) for g in range(cp)]
                + [page_spec((ps * fh, fd), g) for g in range(cp)] + [page_spec((ps * fh, fd), g) for g in range(cp)]
                + [page_spec((fh, ps), g) for g in range(cp)]
                + [seq((2 * ds * hm, kvl)), seq((ds * hm, rd)), seq((fh * _SUBLANES, fd)), seq((fh * _SUBLANES, _LANES)),
                   seq((fh, ps)), seq((ps, kvl)), seq((rd, ps)), seq((ps * fh, fd)), seq((ps * fh, fd)),
                   cst(wukt.shape), cst(wuv_flat.shape), cst(hmask.shape)])
    rm, rf = ds * _SUBLANES, fh * _SUBLANES
    o_mla_s, o_fox_s = pl.pallas_call(
        functools.partial(_sample_body, cp=cp, hm=hm, fh=fh, ds=ds, kvl=kvl, rd=rd, nope=nope,
                          mla_scale=mla_scale, fox_scale=fox_scale),
        grid_spec=pltpu.PrefetchScalarGridSpec(
            num_scalar_prefetch=1, grid=(db, nc), in_specs=in_specs,
            out_specs=[seq((ds, hm * dv)), seq((rf, fd))],
            scratch_shapes=[pltpu.VMEM((rm, _LANES), _F32), pltpu.VMEM((rm, _LANES), _F32), pltpu.VMEM((rm, kvl), _F32),
                            pltpu.VMEM((rf, _LANES), _F32), pltpu.VMEM((rf, _LANES), _F32), pltpu.VMEM((rf, fd), _F32),
                            pltpu.VMEM((_SUBLANES, ps), _F32), pltpu.VMEM((cp * ps, kvl), _BF)]),
        out_shape=[sds((db, ds, hm * dv), _F32), sds((db, rf, fd), _F32)],
        compiler_params=_params(("parallel", "arbitrary")),
        name="sample_attn",
    )(pt, *([cache_lat] * cp), *([kr_t] * cp), *([fk4] * cp), *([fv4] * cp), *([lf_t] * cp),
      qlat, qrr, qf_rows, crow, ckey, lat_new, kr_new, fk_new, fv_new, wukt, wuv_flat, hmask)

    mk4 = cache_mk.reshape(cache_mk.shape[0], db, mlen * mh, md)
    mv4 = cache_mv.reshape(cache_mv.shape[0], db, mlen * mh, md)
    o_mem_s = pl.pallas_call(
        functools.partial(_smem_body, mh=mh, keys=mlen, scale=mem_scale),
        grid=(db,),
        in_specs=[pl.BlockSpec((None, rf, md), lambda b: (b, 0, 0)),
                  pl.BlockSpec((None, None, mlen * mh, md), lambda b: (l, b, 0, 0)),
                  pl.BlockSpec((None, None, mlen * mh, md), lambda b: (l, b, 0, 0))],
        out_specs=pl.BlockSpec((None, rf, md), lambda b: (b, 0, 0)),
        out_shape=sds((db, rf, md), _F32),
        compiler_params=_params(("parallel",)),
        name="memory_attn_sample",
    )(qe_rows, mk4, mv4)

    def from_head_rows(a):
        a = a.reshape(db, fh, _SUBLANES, fd)[:, :, 0:ds]
        return jnp.transpose(a, (0, 2, 1, 3)).reshape(ns, fh * fd)
    o_mla = jnp.concatenate([o_mla_p, o_mla_s.reshape(ns, hm * dv).astype(_BF)], axis=0)
    o_fox = jnp.concatenate([o_fox_p, from_head_rows(o_fox_s).astype(_BF)], axis=0)
    o_mem = jnp.concatenate([o_mem_p, from_head_rows(o_mem_s).astype(_BF)], axis=0)

    wr = jnp.zeros((d, _LANES), _F32).at[:, 0:ng].set(w["w_router_group"][l]).at[:, ng:ng + ne].set(w["w_router_expert"][l])
    br = jnp.zeros((1, _LANES), _F32).at[0, 0:ng].set(w["b_router_group"][l]).at[0, ng:ng + ne].set(w["b_router_expert"][l])
    tmm = _tile(n, 256)
    tokm = lambda width: pl.BlockSpec((tmm, width), lambda i: (i, 0))
    wbm, wbf, wbe, wo = (w["w_br_mla"][l].astype(_BF), w["w_br_fox"][l].astype(_BF), w["w_br_mem"][l].astype(_BF),
                         w["w_o"][l].astype(_BF))
    h_all, hn_all, route = pl.pallas_call(
        functools.partial(_merge_body, d=d, ng=ng, epg=epg),
        grid=(n // tmm,),
        in_specs=[tokm(d), tokm(3 * d), tokm(hm * dv), tokm(fh * fd), tokm(mh * md), _const(wbm.shape), _const(wbf.shape),
                  _const(wbe.shape), _const(wo.shape), _const((1, d)), _const(wr.shape), _const(br.shape)],
        out_specs=[tokm(d), tokm(d), tokm(_LANES)],
        out_shape=[sds((n, d), _F32), sds((n, d), _F32), sds((n, _LANES), _F32)],
        compiler_params=_params(("parallel",)),
        name="merge_router",
    )(x_all, gates, o_mla, o_fox, o_mem, wbm, wbf, wbe, wo, row2(w["g_ffn_norm"][l]), wr, br)

    te_m = _LANES
    pair_e = route[:, 0:2].astype(jnp.int32).reshape(-1)
    onehot = (pair_e[:, None] == jnp.arange(ne)[None, :]).astype(jnp.int32)
    counts = jnp.sum(onehot, axis=0)
    rank = jnp.sum((jnp.cumsum(onehot, axis=0) - 1) * onehot, axis=1)
    padded = -(-counts // te_m) * te_m
    ends = jnp.cumsum(padded)
    starts = ends - padded
    dest = starts[pair_e] + rank
    p_pad = 2 * n + ne * te_m
    n_tiles = p_pad // te_m
    row_tok = jnp.zeros((p_pad,), jnp.int32).at[dest].set(jnp.arange(2 * n, dtype=jnp.int32) // 2)
    tile_e = jnp.minimum(jnp.searchsorted(ends, jnp.arange(n_tiles, dtype=jnp.int32) * te_m, side="right"), ne - 1)
    tile_e = tile_e.astype(jnp.int32)
    n_active = (ends[-1] // te_m).astype(jnp.int32).reshape(1)

    y_pairs = pl.pallas_call(
        _moe_body,
        grid_spec=pltpu.PrefetchScalarGridSpec(
            num_scalar_prefetch=3, grid=(n_tiles,),
            in_specs=[pl.BlockSpec(memory_space=pl.ANY),
                      pl.BlockSpec((None, None, d, dff), lambda i, te, rt, na: (l, te[i], 0, 0)),
                      pl.BlockSpec((None, None, d, dff), lambda i, te, rt, na: (l, te[i], 0, 0)),
                      pl.BlockSpec((None, None, dff, d), lambda i, te, rt, na: (l, te[i], 0, 0))],
            out_specs=pl.BlockSpec((te_m, d), lambda i, te, rt, na: (i, 0)),
            scratch_shapes=[pltpu.VMEM((te_m, d), _F32), pltpu.SemaphoreType.DMA(())]),
        out_shape=sds((p_pad, d), _F32),
        compiler_params=_params(("arbitrary",)),
        name="experts",
    )(tile_e, row_tok, n_active, hn_all, w["w_e_gate"], w["w_e_up"], w["w_e_down"])

    tc = _tile(n, 128)
    y_all = pl.pallas_call(
        _combine_body,
        grid_spec=pltpu.PrefetchScalarGridSpec(
            num_scalar_prefetch=1, grid=(n // tc,),
            in_specs=[pl.BlockSpec(memory_space=pl.ANY),
                      pl.BlockSpec((tc, d), lambda i, pos: (i, 0)),
                      pl.BlockSpec((tc, _LANES), lambda i, pos: (i, 0))],
            out_specs=pl.BlockSpec((tc, d), lambda i, pos: (i, 0)),
            scratch_shapes=[pltpu.VMEM((2, tc, d), _F32), pltpu.SemaphoreType.DMA(())]),
        out_shape=sds((n, d), _F32),
        compiler_params=_params(("arbitrary",)),
        name="combine",
    )(dest.astype(jnp.int32), y_pairs, h_all, route)

    new_rows = dict(lat=lat, krope=krope, kfox=kfox, vfox=vfox, logf=lfr[:, 0:fh], mem_k=mem_k, mem_v=mem_v)
    return y_all, new_rows


def kernel(x_prompt, x_sample, cache_mla_latent, cache_mla_krope, cache_fox_k, cache_fox_v, cache_fox_logf, cache_mem_k, cache_mem_v, page_table, mem_prompt, g_attn_norm, w_in, b_f, g_cq, w_uq, g_ckv, w_uk, w_uv, g_q_mla, g_k_mla, g_q_fox, g_k_fox, g_mem_norm, w_mem_kv, g_q_mem, g_k_mem, w_br_mla, w_br_fox, w_br_mem, w_o, g_ffn_norm, w_router_group, b_router_group, w_router_expert, b_router_expert, w_e_gate, w_e_up, w_e_down):
    w = dict(g_attn_norm=g_attn_norm, w_in=w_in, b_f=b_f, g_cq=g_cq, w_uq=w_uq, g_ckv=g_ckv, w_uk=w_uk, w_uv=w_uv,
             g_q_mla=g_q_mla, g_k_mla=g_k_mla, g_q_fox=g_q_fox, g_k_fox=g_k_fox, g_mem_norm=g_mem_norm,
             w_mem_kv=w_mem_kv, g_q_mem=g_q_mem, g_k_mem=g_k_mem, w_br_mla=w_br_mla, w_br_fox=w_br_fox,
             w_br_mem=w_br_mem, w_o=w_o, g_ffn_norm=g_ffn_norm, w_router_group=w_router_group,
             b_router_group=b_router_group, w_router_expert=w_router_expert, b_router_expert=b_router_expert,
             w_e_gate=w_e_gate, w_e_up=w_e_up, w_e_down=w_e_down)
    bsz, t, d = x_prompt.shape
    db, ds, _ = x_sample.shape
    depth = w_in.shape[0]
    assert bsz == 1, "the prompt group is a single sequence"
    fh, fd = cache_fox_k.shape[3], cache_fox_k.shape[4]
    mh, md = cache_mem_k.shape[3], cache_mem_k.shape[4]
    mlen = mem_prompt.shape[1]
    x_all = jnp.concatenate([x_prompt.reshape(t, d), x_sample.reshape(db * ds, d)], axis=0)
    caches = (cache_mla_latent, cache_mla_krope, cache_fox_k, cache_fox_v, cache_fox_logf, cache_mem_k, cache_mem_v)
    rows = []
    for l in range(depth):
        x_all, r = _layer(l, x_all, t, caches, page_table, mem_prompt[0], w)
        rows.append(r)
    st = lambda key, sl, shape: jnp.stack([r[key][sl].reshape(shape) for r in rows])
    p, s = slice(0, t), slice(t, None)
    kvl = cache_mla_latent.shape[3]
    rd = cache_mla_krope.shape[3]
    return (x_all[p].reshape(1, t, d), x_all[s].reshape(db, ds, d),
            st("lat", p, (1, t, kvl)), st("krope", p, (1, t, rd)), st("kfox", p, (1, t, fh, fd)),
            st("vfox", p, (1, t, fh, fd)), st("logf", p, (1, t, fh)),
            jnp.stack([r["mem_k"].reshape(1, mlen, mh, md) for r in rows]),
            jnp.stack([r["mem_v"].reshape(1, mlen, mh, md) for r in rows]),
            st("lat", s, (db, ds, kvl)), st("krope", s, (db, ds, rd)), st("kfox", s, (db, ds, fh, fd)),
            st("vfox", s, (db, ds, fh, fd)), st("logf", s, (db, ds, fh)))
```

```python
import functools

import jax
import jax.numpy as jnp
from jax import lax
from jax.experimental import pallas as pl
from jax.experimental.pallas import tpu as pltpu

_BF = jnp.bfloat16
_F32 = jnp.float32
_EPS = 1e-6
_NEG_INF = -1e30
_ROPE_THETA = 10000.0
_LANES = 128
_SUBLANES = 8
_VMEM_LIMIT = 56 * 1024 * 1024


def _dot(a, b):
    return jnp.dot(a, b, preferred_element_type=_F32)


def _dot_nt(a, b):
    return lax.dot_general(a, b, (((1,), (1,)), ((), ())), preferred_element_type=_F32)


def _split3(x):
    hi = x.astype(_BF)
    r1 = x - hi.astype(_F32)
    mid = r1.astype(_BF)
    lo = (r1 - mid.astype(_F32)).astype(_BF)
    return hi, mid, lo


def _dot01(m01, x):
    hi, mid, lo = _split3(x)
    return _dot(m01, hi) + _dot(m01, mid) + _dot(m01, lo)


def _dot_hi(a, b):
    ah = a.astype(_BF)
    al = (a - ah.astype(_F32)).astype(_BF)
    bh = b.astype(_BF)
    bl = (b - bh.astype(_F32)).astype(_BF)
    return _dot(ah, bh) + _dot(al, bh) + _dot(ah, bl)


def _rms(x, g, n=None):
    n = x.shape[-1] if n is None else n
    ms = jnp.sum(x * x, axis=-1, keepdims=True) * (1.0 / n)
    return x * lax.rsqrt(ms + _EPS) * g


def _log_sigmoid(x):
    return jnp.minimum(x, 0.0) - jnp.log1p(jnp.exp(-jnp.abs(x)))


def _sigmoid(x):
    return 1.0 / (1.0 + jnp.exp(-x))


def _lane_prefix(x):
    lane = lax.broadcasted_iota(jnp.int32, x.shape, 1)
    s = 1
    while s < _LANES:
        x = x + jnp.where(lane >= s, pltpu.roll(x, s, 1), 0.0)
        s *= 2
    return x


def _lane_suffix(x):
    lane = lax.broadcasted_iota(jnp.int32, x.shape, 1)
    s = 1
    while s < _LANES:
        x = x + jnp.where(lane < _LANES - s, pltpu.roll(x, _LANES - s, 1), 0.0)
        s *= 2
    return x


def _tile(n, pref, mult=_SUBLANES):
    t = min(pref, n) // mult * mult
    while n % t:
        t -= mult
    return t


def _const(shape):
    nd = len(shape)
    return pl.BlockSpec(shape, lambda *_: (0,) * nd, pipeline_mode=pl.Buffered(1))


def _params(sem):
    return pltpu.CompilerParams(dimension_semantics=sem, vmem_limit_bytes=_VMEM_LIMIT)


def _proj_body(x_ref, gat_ref, wsm_ref, wflt_ref, wflr_ref, bfc_ref, bfr_ref, gcq_ref, gckv_ref,
               wq_ref, wqr_ref, rc_ref, rs_ref, gqm_ref, gqf_ref, gkf_ref, gqe_ref,
               wuk_ref, wuv_ref, gkn_ref, gkr_ref, tri_ref,
               qmla_ref, lat_ref, krope_ref, qfox_ref, kfox_ref, vfox_ref, kfoxb_ref, vfoxb_ref,
               lft_ref, ft_ref, lfr_ref, fr_ref, qmem_ref, kmla_ref, vmla_ref, xn_ref,
               ct_ref, cr_ref, *, dims):
    ql, kvl, rd, nope, hm, fh, fd, mh, md, hp = dims
    tm = x_ref.shape[0]

    @pl.when(pl.program_id(0) == 0)
    def _():
        ct_ref[...] = jnp.zeros_like(ct_ref)
        cr_ref[...] = jnp.zeros_like(cr_ref)

    xb = _rms(x_ref[...], gat_ref[...]).astype(_BF)
    xn_ref[...] = xb
    proj = _dot(xb, wsm_ref[...])
    o = 0
    cq = proj[:, o:o + ql]; o += ql
    ckv = proj[:, o:o + kvl]; o += kvl
    krb = proj[:, o:o + _LANES]; o += _LANES
    krr = proj[:, o:o + _LANES]; o += _LANES
    fq = proj[:, o:o + fh * fd]; o += fh * fd
    fk = proj[:, o:o + fh * fd]; o += fh * fd
    fv = proj[:, o:o + fh * fd]; o += fh * fd
    mq = proj[:, o:o + mh * md]

    rc = rc_ref[...]
    rs = rs_ref[...]
    qk_dim = nope + rd

    cqn = _rms(cq, gcq_ref[...]).astype(_BF)
    q = _dot(cqn, wq_ref[...])
    qr = _dot(cqn, wqr_ref[...])
    for h in range(hm):
        sl = slice(h * hp, (h + 1) * hp)
        qh = q[:, sl] * rc + qr[:, sl] * rs
        qmla_ref[:, sl] = _rms(qh, gqm_ref[...], qk_dim).astype(_BF)

    lat = _rms(ckv, gckv_ref[...])
    lat_ref[...] = lat
    kr = krb * rc[:, nope:nope + _LANES] + krr * rs[:, nope:nope + _LANES]
    krope_ref[...] = kr[:, 0:rd]

    latb = lat.astype(_BF)
    kn = _dot(latb, wuk_ref[...])
    vmla_ref[...] = _dot(latb, wuv_ref[...]).astype(_BF)
    ssr = jnp.sum(kr * kr, axis=-1, keepdims=True)
    for h in range(hm):
        knh = kn[:, h * nope:(h + 1) * nope]
        rinv = lax.rsqrt((jnp.sum(knh * knh, axis=-1, keepdims=True) + ssr) * (1.0 / qk_dim) + _EPS)
        kmla_ref[:, h * hp:h * hp + nope] = (knh * rinv * gkn_ref[...]).astype(_BF)
        kmla_ref[:, h * hp + nope:(h + 1) * hp] = (kr * rinv * gkr_ref[...]).astype(_BF)

    for h in range(fh):
        sl = slice(h * fd, (h + 1) * fd)
        qfox_ref[:, sl] = _rms(fq[:, sl], gqf_ref[...]).astype(_BF)
        kf = _rms(fk[:, sl], gkf_ref[...])
        kfox_ref[:, sl] = kf
        kfoxb_ref[:, sl] = kf.astype(_BF)
    vfox_ref[...] = fv
    vfoxb_ref[...] = fv.astype(_BF)
    for h in range(mh):
        sl = slice(h * md, (h + 1) * md)
        qmem_ref[:, sl] = _rms(mq[:, sl], gqe_ref[...]).astype(_BF)

    row = lax.broadcasted_iota(jnp.int32, (_SUBLANES, tm), 0)
    lft = jnp.where(row < fh, _log_sigmoid(_dot_nt(wflt_ref[...], xb) + bfc_ref[:, 0:1]), 0.0)
    lft_ref[...] = lft
    carry = ct_ref[:, 0:1]
    for j in range(tm // _LANES):
        c = _lane_prefix(lft[:, j * _LANES:(j + 1) * _LANES]) + carry
        ft_ref[:, j * _LANES:(j + 1) * _LANES] = c
        carry = c[:, _LANES - 1:_LANES]
    ct_ref[...] = jnp.broadcast_to(carry, ct_ref.shape)

    lane = lax.broadcasted_iota(jnp.int32, (tm, _LANES), 1)
    lfr = jnp.where(lane < fh, _log_sigmoid(_dot(xb, wflr_ref[...]) + bfr_ref[...]), 0.0)
    lfr_ref[...] = lfr
    fr = _dot01(tri_ref[...], lfr) + cr_ref[0:1, :]
    fr_ref[...] = fr
    cr_ref[...] = jnp.broadcast_to(fr[tm - 1:tm, :], cr_ref.shape)


def _gate_body(xn_ref, wg_ref, o_ref):
    o_ref[...] = _sigmoid(_dot(xn_ref[...], wg_ref[...]))


def _flash_body(*refs, heads, dk, dv, scale, bias):
    if bias:
        q_ref, k_ref, v_ref, fr_ref, ft_ref, o_ref, m_ref, l_ref, acc_ref = refs
    else:
        q_ref, k_ref, v_ref, o_ref, m_ref, l_ref, acc_ref = refs
    qi = pl.program_id(0)
    ki = pl.program_id(1)
    tq = q_ref.shape[0]
    tk = k_ref.shape[0]

    @pl.when(ki == 0)
    def _():
        m_ref[...] = jnp.full_like(m_ref, _NEG_INF)
        l_ref[...] = jnp.zeros_like(l_ref)
        acc_ref[...] = jnp.zeros_like(acc_ref)

    def step(masked):
        if masked:
            keep = (lax.broadcasted_iota(jnp.int32, (tq, tk), 0) >= lax.broadcasted_iota(jnp.int32, (tq, tk), 1))
        qk = lambda h: _dot_nt(q_ref[:, h * dk:(h + 1) * dk], k_ref[:, h * dk:(h + 1) * dk])
        s_next = qk(0)
        for h in range(heads):
            s = s_next * scale
            if h + 1 < heads:
                s_next = qk(h + 1)
            if bias:
                s = s + fr_ref[:, h:h + 1] - ft_ref[h:h + 1, :]
            if masked:
                s = jnp.where(keep, s, _NEG_INF)
            m_prev = m_ref[h]
            m_new = jnp.maximum(m_prev, jnp.max(s, axis=-1, keepdims=True))
            alpha = jnp.exp(m_prev - m_new)
            p = jnp.exp(s - m_new)
            l_ref[h] = alpha * l_ref[h] + jnp.sum(p, axis=-1, keepdims=True)
            acc_ref[h] = alpha * acc_ref[h] + _dot(p.astype(_BF), v_ref[:, h * dv:(h + 1) * dv])
            m_ref[h] = m_new

    @pl.when(ki < qi)
    def _():
        step(False)

    @pl.when(ki == qi)
    def _():
        step(True)
        for h in range(heads):
            o_ref[:, h * dv:(h + 1) * dv] = (acc_ref[h] / l_ref[h]).astype(o_ref.dtype)


def _flash(q, k, v, fr, ft, t, heads, dk, dv, scale, tq):
    nq = t // tq
    bias = fr is not None
    kmap = lambda i, j: (jnp.minimum(i, j), 0)
    in_specs = [pl.BlockSpec((tq, heads * dk), lambda i, j: (i, 0)),
                pl.BlockSpec((tq, heads * dk), kmap),
                pl.BlockSpec((tq, heads * dv), kmap)]
    args = [q, k, v]
    if bias:
        in_specs += [pl.BlockSpec((tq, _LANES), lambda i, j: (i, 0)),
                     pl.BlockSpec((_SUBLANES, tq), lambda i, j: (0, jnp.minimum(i, j)))]
        args += [fr, ft]
    return pl.pallas_call(
        functools.partial(_flash_body, heads=heads, dk=dk, dv=dv, scale=scale, bias=bias),
        grid=(nq, nq),
        in_specs=in_specs,
        out_specs=pl.BlockSpec((tq, heads * dv), lambda i, j: (i, 0)),
        out_shape=jax.ShapeDtypeStruct((t, heads * dv), _BF),
        scratch_shapes=[pltpu.VMEM((heads, tq, 1), _F32), pltpu.VMEM((heads, tq, 1), _F32),
                        pltpu.VMEM((heads, tq, dv), _F32)],
        compiler_params=_params(("parallel", "arbitrary")),
        name="flash_bias" if bias else "flash_mla",
    )(*args)


def _memkv_body(mem_ref, g_ref, w_ref, gk_ref, k_ref, v_ref, kb_ref, vb_ref, *, heads, d):
    mn = _rms(mem_ref[...], g_ref[...]).astype(_BF)
    kv = _dot(mn, w_ref[...])
    for h in range(heads):
        sl = slice(h * d, (h + 1) * d)
        k = _rms(kv[:, sl], gk_ref[...])
        k_ref[:, sl] = k
        kb_ref[:, sl] = k.astype(_BF)
    v = kv[:, heads * d:]
    v_ref[...] = v
    vb_ref[...] = v.astype(_BF)


def _memattn_body(q_ref, k_ref, v_ref, o_ref, *, heads, d, scale):
    for h in range(heads):
        sl = slice(h * d, (h + 1) * d)
        s = _dot_nt(q_ref[:, sl], k_ref[:, sl]) * scale
        p = jnp.exp(s - jnp.max(s, axis=-1, keepdims=True))
        o = _dot(p.astype(_BF), v_ref[:, sl]) / jnp.sum(p, axis=-1, keepdims=True)
        o_ref[:, sl] = o.astype(o_ref.dtype)


def _sprep_body(q_ref, gkn_ref, gkr_ref, wukh_ref, lfr_ref, g4_ref, qhi_ref, qlo_ref, qr_ref, c_ref, *, hm, nope, rd, hp):
    for h in range(hm):
        qn = (q_ref[:, h * hp:h * hp + nope].astype(_F32) * gkn_ref[...]).astype(_BF)
        ql = _dot(qn, wukh_ref[h])
        hi = ql.astype(_BF)
        qhi_ref[h] = hi
        qlo_ref[h] = (ql - hi.astype(_F32)).astype(_BF)
        qr_ref[h] = (q_ref[:, h * hp + nope:h * hp + nope + _LANES].astype(_F32) * gkr_ref[...]).astype(_BF)
    c_ref[...] = _dot01(g4_ref[...], lfr_ref[...])


def _sample_body(pt_ref, clat_hbm, ckr_hbm, cfk_hbm, cfv_hbm, clf_hbm,
                 qlat_ref, qr_ref, qf_ref, crow_ref, ckey_ref, latn_ref, krn_ref, fkn_ref, fvn_ref,
                 wukt_ref, wuv_ref, hmask_ref, omla_ref, ofox_ref,
                 lat_buf, kr_buf, fk_buf, fv_buf, lf_buf, sem, wq_ref,
                 mm_ref, lm_ref, am_ref, mf_ref, lf_ref, af_ref, car_ref, latb_ref,
                 *, layer, cp, nc, hm, fh, ds, rd, nope, mla_scale, fox_scale):
    b = pl.program_id(0)
    nb = pl.num_programs(0)
    npg = nc * cp
    ps = _LANES
    rows_m = ds * _SUBLANES
    rows_f = fh * _SUBLANES
    nw = hm * nope
    qk_dim = nope + rd
    rid8 = lax.broadcasted_iota(jnp.int32, (_SUBLANES, 1), 0)

    def page_copies(step, slot):
        sb = step // nc
        base = sb * npg + (nc - 1 - (step - sb * nc)) * cp
        out = []
        for g in range(cp):
            page = pt_ref[base + g]
            for src, dst in ((clat_hbm, lat_buf), (ckr_hbm, kr_buf), (cfk_hbm, fk_buf), (cfv_hbm, fv_buf),
                             (clf_hbm, lf_buf)):
                out.append(pltpu.make_async_copy(src.at[layer, page], dst.at[slot, g], sem.at[slot]))
        return out

    def mla_scores(latb, krt):
        n = latb.shape[0]
        allr = _dot_nt(wq_ref[...], latb)
        ssq = jnp.zeros((_SUBLANES, n), _F32)
        for h in range(hm):
            knh = allr[h * nope:(h + 1) * nope]
            ssq = ssq + jnp.where(rid8 == h, jnp.sum(knh * knh, axis=0, keepdims=True), 0.0)
        ssr = jnp.sum(krt * krt, axis=0, keepdims=True)
        rinv = lax.rsqrt((ssq + ssr) * (1.0 / qk_dim) + _EPS)
        s = allr[nw:nw + rows_m] + allr[nw + rows_m:nw + 2 * rows_m] + _dot(qr_ref[...], krt.astype(_BF))
        return s * jnp.concatenate([rinv] * ds, axis=0) * mla_scale

    def fox_scores(k_list, bias_list):
        out = []
        for h in range(fh):
            kh = jnp.concatenate([r[pl.ds(h, ps, stride=fh), :] for r in k_list], axis=0).astype(_BF)
            bh = jnp.concatenate([bb[h:h + 1, :] for bb in bias_list], axis=1)
            qh = qf_ref[h * _SUBLANES:(h + 1) * _SUBLANES, :]
            out.append(_dot_nt(qh, kh) * fox_scale + crow_ref[h * _SUBLANES:(h + 1) * _SUBLANES, 0:1] + bh)
        return jnp.concatenate(out, axis=0)

    def softmax_update(s, m_ref, l_ref):
        m_prev = m_ref[:, 0:1]
        m_new = jnp.maximum(m_prev, jnp.max(s, axis=-1, keepdims=True))
        alpha = jnp.exp(m_prev - m_new)
        p = jnp.exp(s - m_new)
        l_ref[...] = jnp.broadcast_to(alpha * l_ref[:, 0:1] + jnp.sum(p, axis=-1, keepdims=True), l_ref.shape)
        m_ref[...] = jnp.broadcast_to(m_new, m_ref.shape)
        return alpha, p

    def fox_pv(p, v_list, off):
        n = len(v_list) * ps
        for h in range(fh):
            vh = jnp.concatenate([r[pl.ds(h, ps, stride=fh), :] for r in v_list], axis=0).astype(_BF)
            ph = p[h * _SUBLANES:(h + 1) * _SUBLANES, off:off + n].astype(_BF)
            af_ref[h * _SUBLANES:(h + 1) * _SUBLANES, :] += _dot(ph, vh)

    @pl.when(b == 0)
    def _():
        wq_ref[0:nw, :] = wukt_ref[...]
        for cpy in page_copies(0, 0):
            cpy.start()

    wq_ref[nw:nw + 2 * rows_m, :] = qlat_ref[...]
    mm_ref[...] = jnp.full_like(mm_ref, _NEG_INF)
    lm_ref[...] = jnp.zeros_like(lm_ref)
    am_ref[...] = jnp.zeros_like(am_ref)
    mf_ref[...] = jnp.full_like(mf_ref, _NEG_INF)
    lf_ref[...] = jnp.zeros_like(lf_ref)
    af_ref[...] = jnp.zeros_like(af_ref)
    car_ref[...] = jnp.zeros_like(car_ref)
    latb = latn_ref[...].astype(_BF)
    key = lax.broadcasted_iota(jnp.int32, (rows_m, ps), 1)
    tok = lax.broadcasted_iota(jnp.int32, (rows_m, ps), 0) >> 3
    s = jnp.where((key < ds) & (key <= tok), mla_scores(latb, krn_ref[...]), _NEG_INF)
    alpha, p = softmax_update(s, mm_ref, lm_ref)
    am_ref[...] = alpha * am_ref[...] + _dot(p.astype(_BF), latb)
    key = lax.broadcasted_iota(jnp.int32, (rows_f, ps), 1)
    tok = lax.broadcasted_iota(jnp.int32, (rows_f, ps), 0) & (_SUBLANES - 1)
    sf = jnp.where((key < ds) & (key <= tok), fox_scores([fkn_ref], [-ckey_ref[...]]), _NEG_INF)
    alpha, p = softmax_update(sf, mf_ref, lf_ref)
    af_ref[...] = alpha * af_ref[...]
    fox_pv(p, [fvn_ref], 0)

    def chunk(step, slot):
        for cpy in page_copies(step, slot):
            cpy.wait()

        @pl.when(step + 1 < nb * nc)
        def _():
            for cpy in page_copies(step + 1, 1 - slot):
                cpy.start()

        carry = car_ref[0:fh, :]
        after = [None] * cp
        for g in reversed(range(cp)):
            x = lf_buf[slot, g]
            suf = _lane_suffix(x)
            after[g] = carry + suf - x
            carry = carry + suf[:, 0:1]
        car_ref[0:fh, :] = carry

        sm = []
        sf = []
        for g in range(0, cp, 2):
            latb = jnp.concatenate([lat_buf[slot, g], lat_buf[slot, g + 1]], axis=0).astype(_BF)
            latb_ref[g * ps:(g + 2) * ps, :] = latb
            krt = jnp.concatenate([kr_buf[slot, g], kr_buf[slot, g + 1]], axis=1)
            sm.append(mla_scores(latb, krt))
            sf.append(fox_scores([fk_buf.at[slot, g], fk_buf.at[slot, g + 1]], after[g:g + 2]))
        alpha, p = softmax_update(jnp.concatenate(sm, axis=1), mm_ref, lm_ref)
        acc = alpha * am_ref[...]
        for g in range(0, cp, 2):
            acc = acc + _dot(p[:, g * ps:(g + 2) * ps].astype(_BF), latb_ref[g * ps:(g + 2) * ps, :])
        am_ref[...] = acc
        alpha, p = softmax_update(jnp.concatenate(sf, axis=1), mf_ref, lf_ref)
        af_ref[...] = alpha * af_ref[...]
        for g in range(0, cp, 2):
            fox_pv(p, [fv_buf.at[slot, g], fv_buf.at[slot, g + 1]], g * ps)

    def two_chunks(i, carry):
        chunk(b * nc + 2 * i, 0)
        chunk(b * nc + 2 * i + 1, 1)
        return carry
    lax.fori_loop(0, nc // 2, two_chunks, 0)

    accn = (am_ref[...] / lm_ref[:, 0:1]).astype(_BF)
    o = _dot(accn, wuv_ref[...])
    for t in range(ds):
        blk = jnp.where(hmask_ref[...] > 0, o[t * _SUBLANES:(t + 1) * _SUBLANES], 0.0)
        omla_ref[t:t + 1, :] = jnp.sum(blk, axis=0, keepdims=True)
    ofox_ref[...] = af_ref[...] / lf_ref[:, 0:1]


def _smem_body(q_ref, k_ref, v_ref, o_ref, *, mh, keys, scale):
    for h in range(mh):
        kh = k_ref[pl.ds(h, keys, stride=mh), :].astype(_BF)
        vh = v_ref[pl.ds(h, keys, stride=mh), :].astype(_BF)
        s = _dot_nt(q_ref[h * _SUBLANES:(h + 1) * _SUBLANES, :], kh) * scale
        p = jnp.exp(s - jnp.max(s, axis=-1, keepdims=True))
        o_ref[h * _SUBLANES:(h + 1) * _SUBLANES, :] = _dot(p.astype(_BF), vh) / jnp.sum(p, axis=-1, keepdims=True)


def _merge_body(x_ref, g_ref, om_ref, of_ref, oe_ref, wbm_ref, wbf_ref, wbe_ref, wo_ref, gffn_ref, wr_ref, br_ref,
                h_ref, hn_ref, route_ref, *, d, ng, epg):
    g = g_ref[...]
    merged = (g[:, 0:d] * _dot(om_ref[...], wbm_ref[...]) + g[:, d:2 * d] * _dot(of_ref[...], wbf_ref[...])
              + g[:, 2 * d:3 * d] * _dot(oe_ref[...], wbe_ref[...]))
    h = x_ref[...] + _dot(merged.astype(_BF), wo_ref[...])
    h_ref[...] = h
    hn = _rms(h, gffn_ref[...])
    hn_ref[...] = hn
    logits = _dot_hi(hn, wr_ref[...]) + br_ref[...]
    tm = logits.shape[0]
    lane = lax.broadcasted_iota(jnp.int32, (tm, _LANES), 1).astype(_F32)
    big = float(4 * _LANES)
    isg = lane < ng
    lg = jnp.where(isg, logits, _NEG_INF)
    eg = jnp.where(isg, jnp.exp(lg - jnp.max(lg, axis=-1, keepdims=True)), 0.0)
    pg = eg / jnp.sum(eg, axis=-1, keepdims=True)
    pg_max = jnp.max(pg, axis=-1, keepdims=True)
    gidx = jnp.min(jnp.where(isg & (pg == pg_max), lane, big), axis=-1, keepdims=True)
    lo = ng + gidx * epg
    ine = (lane >= lo) & (lane < lo + epg)
    le = jnp.where(ine, logits, _NEG_INF)
    ee = jnp.where(ine, jnp.exp(le - jnp.max(le, axis=-1, keepdims=True)), 0.0)
    pe = ee / jnp.sum(ee, axis=-1, keepdims=True)
    p1 = jnp.max(jnp.where(ine, pe, -1.0), axis=-1, keepdims=True)
    i1 = jnp.min(jnp.where(ine & (pe == p1), lane, big), axis=-1, keepdims=True)
    rest = ine & (lane != i1)
    p2 = jnp.max(jnp.where(rest, pe, -1.0), axis=-1, keepdims=True)
    i2 = jnp.min(jnp.where(rest & (pe == p2), lane, big), axis=-1, keepdims=True)
    den = p1 + p2
    w1 = pg_max * p1 / den
    w2 = pg_max * p2 / den
    route_ref[...] = jnp.where(lane == 0, i1 - ng, jnp.where(lane == 1, i2 - ng,
                                                             jnp.where(lane == 2, w1, jnp.where(lane == 3, w2, 0.0))))


def _moe_body(te_ref, rt_ref, na_ref, hn_hbm, wg_ref, wu_ref, wd_ref, y_ref, xbuf, sem):
    i = pl.program_id(0)
    na = na_ref[0]
    tm = xbuf.shape[1]
    slot = lax.rem(i, 2)

    def row_copy(slot_, r, tok):
        return pltpu.make_async_copy(hn_hbm.at[pl.ds(tok, 1)], xbuf.at[slot_, pl.ds(r, 1)], sem.at[slot_])

    def gather(tile, slot_):
        for r in range(tm):
            row_copy(slot_, r, rt_ref[tile * tm + r]).start()

    @pl.when((i == 0) & (na > 0))
    def _():
        gather(0, 0)

    @pl.when(i < na)
    def _():
        for r in range(tm):
            row_copy(slot, r, 0).wait()

        @pl.when(i + 1 < na)
        def _():
            gather(i + 1, 1 - slot)
        xb = xbuf[slot].astype(_BF)
        gate = _dot(xb, wg_ref[...].astype(_BF))
        up = _dot(xb, wu_ref[...].astype(_BF))
        hmid = gate * _sigmoid(gate) * up
        y_ref[...] = _dot(hmid.astype(_BF), wd_ref[...].astype(_BF))

    @pl.when(i >= na)
    def _():
        y_ref[...] = jnp.zeros_like(y_ref)


def _combine_body(pos_ref, y_hbm, h_ref, route_ref, o_ref, ybuf, sem):
    i = pl.program_id(0)
    nt = pl.num_programs(0)
    tm = h_ref.shape[0]
    slot = lax.rem(i, 2)

    def row_copy(slot_, k, r, src):
        return pltpu.make_async_copy(y_hbm.at[pl.ds(src, 1)], ybuf.at[slot_, k, pl.ds(r, 1)], sem.at[slot_])

    def gather(tile, slot_):
        for r in range(tm):
            for k in range(2):
                row_copy(slot_, k, r, pos_ref[2 * (tile * tm + r) + k]).start()

    @pl.when(i == 0)
    def _():
        gather(0, 0)

    for r in range(tm):
        for k in range(2):
            row_copy(slot, k, r, 0).wait()

    @pl.when(i + 1 < nt)
    def _():
        gather(i + 1, 1 - slot)
    rt = route_ref[...]
    o_ref[...] = h_ref[...] + (rt[:, 2:3] * ybuf[slot, 0] + rt[:, 3:4] * ybuf[slot, 1])


def _layer(l, x_all, t, caches, page_table, mem_prompt, w):
    (cache_lat, cache_kr, cache_fk, cache_fv, cache_lf, cache_mk, cache_mv) = caches
    n, d = x_all.shape
    db, npg = page_table.shape
    ds = (n - t) // db
    ps = cache_lat.shape[2]
    kvl = cache_lat.shape[3]
    rd = cache_kr.shape[3]
    fh, fd = cache_fk.shape[3], cache_fk.shape[4]
    mh, md = cache_mk.shape[3], cache_mk.shape[4]
    mlen = cache_mk.shape[2]
    hm, nope = w["w_uk"].shape[2], w["w_uk"].shape[3]
    dv = w["w_uv"].shape[3]
    ql = w["w_uq"].shape[1]
    qk_dim = nope + rd
    hp = -(-qk_dim // _LANES) * _LANES
    ng = w["w_router_group"].shape[2]
    ne = w["w_router_expert"].shape[2]
    epg = ne // ng
    dff = w["w_e_gate"].shape[3]
    assert ps == _LANES and nope == _LANES and fd == _LANES and md == _LANES and dv == _LANES
    assert 2 * rd == _LANES and hp == 2 * _LANES and hm == _SUBLANES and ds <= _SUBLANES
    assert ng + ne <= _LANES and fh <= _SUBLANES and mh == fh

    mla_scale = float(qk_dim) ** -0.5
    fox_scale = float(fd) ** -0.5
    mem_scale = float(md) ** -0.5

    w_in = w["w_in"][l]
    widths = (ql, kvl, rd, fh * fd, fh * fd, fh * fd, fh, mh * md, 3 * d)
    offs = [0]
    for wd_ in widths:
        offs.append(offs[-1] + wd_)
    seg = lambda i: w_in[:, offs[i]:offs[i + 1]]
    half = rd // 2
    rot = jnp.concatenate([jnp.arange(half, rd), jnp.arange(0, half)])
    zpad = jnp.zeros((d, _LANES - rd), _F32)
    wsm = jnp.concatenate([seg(0), seg(1), seg(2), zpad, seg(2)[:, rot], zpad, seg(3), seg(4), seg(5), seg(7)],
                          axis=1).astype(_BF)
    wfl = seg(6)
    wflt = jnp.zeros((_SUBLANES, d), _F32).at[0:fh].set(wfl.T).astype(_BF)
    wflr = jnp.zeros((d, _LANES), _F32).at[:, 0:fh].set(wfl).astype(_BF)
    bfc = jnp.zeros((_SUBLANES, _LANES), _F32).at[0:fh, :].set(jnp.broadcast_to(w["b_f"][l][:, None], (fh, _LANES)))
    bfr = jnp.zeros((1, _LANES), _F32).at[0, 0:fh].set(w["b_f"][l])
    wgate = seg(8).astype(_BF)
    w_uq = w["w_uq"][l].reshape(ql, hm, qk_dim)
    zq = jnp.zeros((ql, hm, hp - qk_dim), _F32)
    zn = jnp.zeros((ql, hm, nope), _F32)
    wq = jnp.concatenate([w_uq, zq], axis=2).reshape(ql, hm * hp).astype(_BF)
    wqr = jnp.concatenate([zn, w_uq[:, :, nope:][:, :, rot], zq], axis=2).reshape(ql, hm * hp).astype(_BF)
    gqm = jnp.zeros((1, hp), _F32).at[0, 0:qk_dim].set(w["g_q_mla"][l])
    gkn = w["g_k_mla"][l][None, 0:nope]
    gkr = jnp.zeros((1, _LANES), _F32).at[0, 0:rd].set(w["g_k_mla"][l][nope:])
    w_uk = w["w_uk"][l]
    wuk_flat = w_uk.reshape(kvl, hm * nope).astype(_BF)
    wukt = wuk_flat.T
    wukh = jnp.transpose(w_uk, (1, 2, 0)).astype(_BF)
    wuv_flat = w["w_uv"][l].reshape(kvl, hm * dv).astype(_BF)
    row2 = lambda v: v[None, :]

    pos = jnp.concatenate([jnp.arange(t, dtype=_F32), jnp.tile(npg * ps + jnp.arange(ds, dtype=_F32), db)])
    inv = _ROPE_THETA ** (-jnp.arange(half, dtype=_F32) / half)
    ang = pos[:, None] * inv[None, :]
    cos, sin = jnp.cos(ang), jnp.sin(ang)
    zt = jnp.zeros((n, hp - qk_dim), _F32)
    rope_c = jnp.concatenate([jnp.ones((n, nope), _F32), cos, cos, zt], axis=1)
    rope_s = jnp.concatenate([jnp.zeros((n, nope), _F32), -sin, sin, zt], axis=1)

    tm = _tile(n, 256, _LANES)
    tri = jnp.tri(tm, dtype=_BF)
    dims = (ql, kvl, rd, nope, hm, fh, fd, mh, md, hp)
    tok = lambda width: pl.BlockSpec((tm, width), lambda i: (i, 0))
    lan = pl.BlockSpec((_SUBLANES, tm), lambda i: (0, i))
    sds = jax.ShapeDtypeStruct
    proj_out = pl.pallas_call(
        functools.partial(_proj_body, dims=dims),
        grid=(n // tm,),
        in_specs=[tok(d), _const((1, d)), _const(wsm.shape), _const(wflt.shape), _const(wflr.shape),
                  _const(bfc.shape), _const(bfr.shape), _const((1, ql)), _const((1, kvl)),
                  _const(wq.shape), _const(wqr.shape), tok(hp), tok(hp), _const((1, hp)),
                  _const((1, fd)), _const((1, fd)), _const((1, md)),
                  _const(wuk_flat.shape), _const(wuv_flat.shape), _const((1, nope)), _const((1, _LANES)),
                  _const(tri.shape)],
        out_specs=[tok(hm * hp), tok(kvl), tok(rd), tok(fh * fd), tok(fh * fd), tok(fh * fd), tok(fh * fd),
                   tok(fh * fd), lan, lan, tok(_LANES), tok(_LANES), tok(mh * md), tok(hm * hp), tok(hm * dv), tok(d)],
        out_shape=[sds((n, hm * hp), _BF), sds((n, kvl), _F32), sds((n, rd), _F32), sds((n, fh * fd), _BF),
                   sds((n, fh * fd), _F32), sds((n, fh * fd), _F32), sds((n, fh * fd), _BF), sds((n, fh * fd), _BF),
                   sds((_SUBLANES, n), _F32), sds((_SUBLANES, n), _F32), sds((n, _LANES), _F32), sds((n, _LANES), _F32),
                   sds((n, mh * md), _BF), sds((n, hm * hp), _BF), sds((n, hm * dv), _BF), sds((n, d), _BF)],
        scratch_shapes=[pltpu.VMEM((_SUBLANES, _LANES), _F32), pltpu.VMEM((_SUBLANES, _LANES), _F32)],
        compiler_params=_params(("arbitrary",)),
        name="token_proj",
    )(x_all, row2(w["g_attn_norm"][l]), wsm, wflt, wflr, bfc, bfr, row2(w["g_cq"][l]), row2(w["g_ckv"][l]),
      wq, wqr, rope_c, rope_s, gqm, row2(w["g_q_fox"][l]), row2(w["g_k_fox"][l]), row2(w["g_q_mem"][l]),
      wuk_flat, wuv_flat, gkn, gkr, tri)
    (qmla, lat, krope, qfox, kfox, vfox, kfoxb, vfoxb, lft, ft, lfr, fr, qmem, kmla, vmla, xnb) = proj_out

    tg = _tile(n, 512)
    tn = d
    gates = pl.pallas_call(
        _gate_body,
        grid=(3 * d // tn, n // tg),
        in_specs=[pl.BlockSpec((tg, d), lambda j, i: (i, 0)), pl.BlockSpec((d, tn), lambda j, i: (0, j))],
        out_specs=pl.BlockSpec((tg, tn), lambda j, i: (i, j)),
        out_shape=sds((n, 3 * d), _F32),
        compiler_params=_params(("parallel", "parallel")),
        name="branch_gates",
    )(xnb, wgate)

    tq = _tile(t, 512)
    o_mla_p = _flash(qmla, kmla, vmla, None, None, t, hm, hp, dv, mla_scale, tq)
    o_fox_p = _flash(qfox, kfoxb, vfoxb, fr, ft, t, fh, fd, fd, fox_scale, tq)

    wmkv = w["w_mem_kv"][l].astype(_BF)
    mem_k, mem_v, mem_kb, mem_vb = pl.pallas_call(
        functools.partial(_memkv_body, heads=mh, d=md),
        out_shape=[sds((mlen, mh * md), _F32), sds((mlen, mh * md), _F32), sds((mlen, mh * md), _BF), sds((mlen, mh * md), _BF)],
        compiler_params=pltpu.CompilerParams(vmem_limit_bytes=_VMEM_LIMIT),
        name="memory_kv",
    )(mem_prompt, row2(w["g_mem_norm"][l]), wmkv, row2(w["g_k_mem"][l]))
    o_mem_p = pl.pallas_call(
        functools.partial(_memattn_body, heads=mh, d=md, scale=mem_scale),
        grid=(t // tq,),
        in_specs=[pl.BlockSpec((tq, mh * md), lambda i: (i, 0)), _const((mlen, mh * md)), _const((mlen, mh * md))],
        out_specs=pl.BlockSpec((tq, mh * md), lambda i: (i, 0)),
        out_shape=sds((t, mh * md), _BF),
        compiler_params=_params(("parallel",)),
        name="memory_attn_prompt",
    )(qmem, mem_kb, mem_vb)

    ns = n - t
    g4 = (jnp.tri(ns, dtype=_F32) * (jnp.arange(ns)[:, None] // ds == jnp.arange(ns)[None, :] // ds)).astype(_BF)
    qhi, qlo, qrs, cnew = pl.pallas_call(
        functools.partial(_sprep_body, hm=hm, nope=nope, rd=rd, hp=hp),
        out_shape=[sds((hm, ns, kvl), _BF), sds((hm, ns, kvl), _BF), sds((hm, ns, _LANES), _BF), sds((ns, _LANES), _F32)],
        compiler_params=pltpu.CompilerParams(vmem_limit_bytes=_VMEM_LIMIT),
        name="sample_prep",
    )(qmla[t:], gkn, gkr, wukh, lfr[t:], g4)
    to_rows = lambda a: jnp.transpose(a.reshape(hm, db, ds, a.shape[-1]), (1, 2, 0, 3)).reshape(db, ds * hm, a.shape[-1])
    qlat = jnp.concatenate([to_rows(qhi), to_rows(qlo)], axis=1)
    qrr = to_rows(qrs)[:, :, 0:rd]

    def head_rows(a):
        a = jnp.transpose(a.reshape(db, ds, fh, fd), (0, 2, 1, 3))
        return jnp.pad(a, ((0, 0), (0, 0), (0, _SUBLANES - ds), (0, 0))).reshape(db, fh * _SUBLANES, fd)
    qf_rows = head_rows(qfox[t:])
    qe_rows = head_rows(qmem[t:])
    c3 = cnew[:, 0:fh].reshape(db, ds, fh)
    crow = jnp.pad(jnp.transpose(c3, (0, 2, 1)), ((0, 0), (0, 0), (0, _SUBLANES - ds))).reshape(db, fh * _SUBLANES, 1)
    crow = jnp.broadcast_to(crow, (db, fh * _SUBLANES, _LANES))
    ckey = jnp.pad(jnp.transpose(c3, (0, 2, 1)), ((0, 0), (0, 0), (0, ps - ds)))
    pad_keys = lambda a: jnp.pad(a.reshape(db, ds, -1), ((0, 0), (0, ps - ds), (0, 0)))
    lat_new = pad_keys(lat[t:])
    kr_new = jnp.transpose(pad_keys(krope[t:]), (0, 2, 1))
    fk_new = pad_keys(kfox[t:]).reshape(db, ps * fh, fd)
    fv_new = pad_keys(vfox[t:]).reshape(db, ps * fh, fd)
    hmask = (jnp.arange(hm * dv)[None, :] // dv == jnp.arange(hm)[:, None]).astype(_F32)

    n_pool = cache_lat.shape[1]
    kr_t = jnp.swapaxes(cache_kr, 2, 3)
    lf_t = jnp.swapaxes(cache_lf, 2, 3)
    fk4 = cache_fk.reshape(cache_fk.shape[0], n_pool, ps * fh, fd)
    fv4 = cache_fv.reshape(cache_fv.shape[0], n_pool, ps * fh, fd)
    cp = 8 if npg % 16 == 0 else 2
    assert npg % (2 * cp) == 0, "the page pipeline alternates two buffer slots per sequence"
    nc = npg // cp
    pt = page_table.reshape(-1)
    seq = lambda shape: pl.BlockSpec((None,) + shape, lambda b, pt_ref: (b,) + (0,) * len(shape))
    cst = lambda shape: pl.BlockSpec(shape, lambda b, pt_ref: (0,) * len(shape))
    hbm = pl.BlockSpec(memory_space=pl.ANY)
    rm, rf = ds * _SUBLANES, fh * _SUBLANES
    in_specs = [hbm] * 5 + [seq((2 * rm, kvl)), seq((rm, rd)), seq((rf, fd)), seq((rf, _LANES)),
                            seq((fh, ps)), seq((ps, kvl)), seq((rd, ps)), seq((ps * fh, fd)), seq((ps * fh, fd)),
                            cst(wukt.shape), cst(wuv_flat.shape), cst(hmask.shape)]
    o_mla_s, o_fox_s = pl.pallas_call(
        functools.partial(_sample_body, layer=l, cp=cp, nc=nc, hm=hm, fh=fh, ds=ds, rd=rd, nope=nope,
                          mla_scale=mla_scale, fox_scale=fox_scale),
        grid_spec=pltpu.PrefetchScalarGridSpec(
            num_scalar_prefetch=1, grid=(db,), in_specs=in_specs,
            out_specs=[seq((ds, hm * dv)), seq((rf, fd))],
            scratch_shapes=[pltpu.VMEM((2, cp, ps, kvl), _F32), pltpu.VMEM((2, cp, rd, ps), _F32),
                            pltpu.VMEM((2, cp, ps * fh, fd), _F32), pltpu.VMEM((2, cp, ps * fh, fd), _F32),
                            pltpu.VMEM((2, cp, fh, ps), _F32), pltpu.SemaphoreType.DMA((2,)),
                            pltpu.VMEM((hm * nope + 2 * rm, kvl), _BF),
                            pltpu.VMEM((rm, _LANES), _F32), pltpu.VMEM((rm, _LANES), _F32), pltpu.VMEM((rm, kvl), _F32),
                            pltpu.VMEM((rf, _LANES), _F32), pltpu.VMEM((rf, _LANES), _F32), pltpu.VMEM((rf, fd), _F32),
                            pltpu.VMEM((_SUBLANES, ps), _F32), pltpu.VMEM((cp * ps, kvl), _BF)]),
        out_shape=[sds((db, ds, hm * dv), _F32), sds((db, rf, fd), _F32)],
        compiler_params=_params(("arbitrary",)),
        name="sample_attn",
    )(pt, cache_lat, kr_t, fk4, fv4, lf_t,
      qlat, qrr, qf_rows, crow, ckey, lat_new, kr_new, fk_new, fv_new, wukt, wuv_flat, hmask)

    mk4 = cache_mk.reshape(cache_mk.shape[0], db, mlen * mh, md)
    mv4 = cache_mv.reshape(cache_mv.shape[0], db, mlen * mh, md)
    o_mem_s = pl.pallas_call(
        functools.partial(_smem_body, mh=mh, keys=mlen, scale=mem_scale),
        grid=(db,),
        in_specs=[pl.BlockSpec((None, rf, md), lambda b: (b, 0, 0)),
                  pl.BlockSpec((None, None, mlen * mh, md), lambda b: (l, b, 0, 0)),
                  pl.BlockSpec((None, None, mlen * mh, md), lambda b: (l, b, 0, 0))],
        out_specs=pl.BlockSpec((None, rf, md), lambda b: (b, 0, 0)),
        out_shape=sds((db, rf, md), _F32),
        compiler_params=_params(("parallel",)),
        name="memory_attn_sample",
    )(qe_rows, mk4, mv4)

    def from_head_rows(a):
        a = a.reshape(db, fh, _SUBLANES, fd)[:, :, 0:ds]
        return jnp.transpose(a, (0, 2, 1, 3)).reshape(ns, fh * fd)
    o_mla = jnp.concatenate([o_mla_p, o_mla_s.reshape(ns, hm * dv).astype(_BF)], axis=0)
    o_fox = jnp.concatenate([o_fox_p, from_head_rows(o_fox_s).astype(_BF)], axis=0)
    o_mem = jnp.concatenate([o_mem_p, from_head_rows(o_mem_s).astype(_BF)], axis=0)

    wr = jnp.zeros((d, _LANES), _F32).at[:, 0:ng].set(w["w_router_group"][l]).at[:, ng:ng + ne].set(w["w_router_expert"][l])
    br = jnp.zeros((1, _LANES), _F32).at[0, 0:ng].set(w["b_router_group"][l]).at[0, ng:ng + ne].set(w["b_router_expert"][l])
    tmm = _tile(n, 256)
    tokm = lambda width: pl.BlockSpec((tmm, width), lambda i: (i, 0))
    wbm, wbf, wbe, wo = (w["w_br_mla"][l].astype(_BF), w["w_br_fox"][l].astype(_BF), w["w_br_mem"][l].astype(_BF),
                         w["w_o"][l].astype(_BF))
    h_all, hn_all, route = pl.pallas_call(
        functools.partial(_merge_body, d=d, ng=ng, epg=epg),
        grid=(n // tmm,),
        in_specs=[tokm(d), tokm(3 * d), tokm(hm * dv), tokm(fh * fd), tokm(mh * md), _const(wbm.shape), _const(wbf.shape),
                  _const(wbe.shape), _const(wo.shape), _const((1, d)), _const(wr.shape), _const(br.shape)],
        out_specs=[tokm(d), tokm(d), tokm(_LANES)],
        out_shape=[sds((n, d), _F32), sds((n, d), _F32), sds((n, _LANES), _F32)],
        compiler_params=_params(("parallel",)),
        name="merge_router",
    )(x_all, gates, o_mla, o_fox, o_mem, wbm, wbf, wbe, wo, row2(w["g_ffn_norm"][l]), wr, br)

    te_m = _LANES
    pair_e = route[:, 0:2].astype(jnp.int32).reshape(-1)
    onehot = (pair_e[:, None] == jnp.arange(ne)[None, :]).astype(jnp.int32)
    counts = jnp.sum(onehot, axis=0)
    rank = jnp.sum((jnp.cumsum(onehot, axis=0) - 1) * onehot, axis=1)
    padded = -(-counts // te_m) * te_m
    ends = jnp.cumsum(padded)
    starts = ends - padded
    dest = starts[pair_e] + rank
    p_pad = 2 * n + ne * te_m
    n_tiles = p_pad // te_m
    row_tok = jnp.zeros((p_pad,), jnp.int32).at[dest].set(jnp.arange(2 * n, dtype=jnp.int32) // 2)
    tile_start = jnp.arange(n_tiles, dtype=jnp.int32) * te_m
    tile_e = jnp.minimum(jnp.sum((ends[None, :] <= tile_start[:, None]).astype(jnp.int32), axis=1), ne - 1)
    n_active = (ends[-1] // te_m).astype(jnp.int32).reshape(1)

    y_pairs = pl.pallas_call(
        _moe_body,
        grid_spec=pltpu.PrefetchScalarGridSpec(
            num_scalar_prefetch=3, grid=(n_tiles,),
            in_specs=[pl.BlockSpec(memory_space=pl.ANY),
                      pl.BlockSpec((None, None, d, dff), lambda i, te, rt, na: (l, te[i], 0, 0)),
                      pl.BlockSpec((None, None, d, dff), lambda i, te, rt, na: (l, te[i], 0, 0)),
                      pl.BlockSpec((None, None, dff, d), lambda i, te, rt, na: (l, te[i], 0, 0))],
            out_specs=pl.BlockSpec((te_m, d), lambda i, te, rt, na: (i, 0)),
            scratch_shapes=[pltpu.VMEM((2, te_m, d), _F32), pltpu.SemaphoreType.DMA((2,))]),
        out_shape=sds((p_pad, d), _F32),
        compiler_params=_params(("arbitrary",)),
        name="experts",
    )(tile_e, row_tok, n_active, hn_all, w["w_e_gate"], w["w_e_up"], w["w_e_down"])

    tc = _tile(n, 128)
    y_all = pl.pallas_call(
        _combine_body,
        grid_spec=pltpu.PrefetchScalarGridSpec(
            num_scalar_prefetch=1, grid=(n // tc,),
            in_specs=[pl.BlockSpec(memory_space=pl.ANY),
                      pl.BlockSpec((tc, d), lambda i, pos: (i, 0)),
                      pl.BlockSpec((tc, _LANES), lambda i, pos: (i, 0))],
            out_specs=pl.BlockSpec((tc, d), lambda i, pos: (i, 0)),
            scratch_shapes=[pltpu.VMEM((2, 2, tc, d), _F32), pltpu.SemaphoreType.DMA((2,))]),
        out_shape=sds((n, d), _F32),
        compiler_params=_params(("arbitrary",)),
        name="combine",
    )(dest.astype(jnp.int32), y_pairs, h_all, route)

    new_rows = dict(lat=lat, krope=krope, kfox=kfox, vfox=vfox, logf=lfr[:, 0:fh], mem_k=mem_k, mem_v=mem_v)
    return y_all, new_rows


def kernel(x_prompt, x_sample, cache_mla_latent, cache_mla_krope, cache_fox_k, cache_fox_v, cache_fox_logf, cache_mem_k, cache_mem_v, page_table, mem_prompt, g_attn_norm, w_in, b_f, g_cq, w_uq, g_ckv, w_uk, w_uv, g_q_mla, g_k_mla, g_q_fox, g_k_fox, g_mem_norm, w_mem_kv, g_q_mem, g_k_mem, w_br_mla, w_br_fox, w_br_mem, w_o, g_ffn_norm, w_router_group, b_router_group, w_router_expert, b_router_expert, w_e_gate, w_e_up, w_e_down):
    w = dict(g_attn_norm=g_attn_norm, w_in=w_in, b_f=b_f, g_cq=g_cq, w_uq=w_uq, g_ckv=g_ckv, w_uk=w_uk, w_uv=w_uv,
             g_q_mla=g_q_mla, g_k_mla=g_k_mla, g_q_fox=g_q_fox, g_k_fox=g_k_fox, g_mem_norm=g_mem_norm,
             w_mem_kv=w_mem_kv, g_q_mem=g_q_mem, g_k_mem=g_k_mem, w_br_mla=w_br_mla, w_br_fox=w_br_fox,
             w_br_mem=w_br_mem, w_o=w_o, g_ffn_norm=g_ffn_norm, w_router_group=w_router_group,
             b_router_group=b_router_group, w_router_expert=w_router_expert, b_router_expert=b_router_expert,
             w_e_gate=w_e_gate, w_e_up=w_e_up, w_e_down=w_e_down)
    bsz, t, d = x_prompt.shape
    db, ds, _ = x_sample.shape
    depth = w_in.shape[0]
    assert bsz == 1, "the prompt group is a single sequence"
    fh, fd = cache_fox_k.shape[3], cache_fox_k.shape[4]
    mh, md = cache_mem_k.shape[3], cache_mem_k.shape[4]
    mlen = mem_prompt.shape[1]
    x_all = jnp.concatenate([x_prompt.reshape(t, d), x_sample.reshape(db * ds, d)], axis=0)
    caches = (cache_mla_latent, cache_mla_krope, cache_fox_k, cache_fox_v, cache_fox_logf, cache_mem_k, cache_mem_v)
    rows = []
    for l in range(depth):
        x_all, r = _layer(l, x_all, t, caches, page_table, mem_prompt[0], w)
        rows.append(r)
    st = lambda key, sl, shape: jnp.stack([r[key][sl].reshape(shape) for r in rows])
    p, s = slice(0, t), slice(t, None)
    kvl = cache_mla_latent.shape[3]
    rd = cache_mla_krope.shape[3]
    return (x_all[p].reshape(1, t, d), x_all[s].reshape(db, ds, d),
            st("lat", p, (1, t, kvl)), st("krope", p, (1, t, rd)), st("kfox", p, (1, t, fh, fd)),
            st("vfox", p, (1, t, fh, fd)), st("logf", p, (1, t, fh)),
            jnp.stack([r["mem_k"].reshape(1, mlen, mh, md) for r in rows]),
            jnp.stack([r["mem_v"].reshape(1, mlen, mh, md) for r in rows]),
            st("lat", s, (db, ds, kvl)), st("krope", s, (db, ds, rd)), st("kfox", s, (db, ds, fh, fd)),
            st("vfox", s, (db, ds, fh, fd)), st("logf", s, (db, ds, fh)))
```

```python
import functools

import jax
import jax.numpy as jnp
from jax import lax
from jax.experimental import pallas as pl
from jax.experimental.pallas import tpu as pltpu

_BF = jnp.bfloat16
_F32 = jnp.float32
_EPS = 1e-6
_NEG_INF = -1e30
_ROPE_THETA = 10000.0
_LANES = 128
_SUBLANES = 8
_VMEM_LIMIT = 56 * 1024 * 1024


def _dot(a, b):
    return jnp.dot(a, b, preferred_element_type=_F32)


def _dot_nt(a, b):
    return lax.dot_general(a, b, (((1,), (1,)), ((), ())), preferred_element_type=_F32)


def _split3(x):
    hi = x.astype(_BF)
    r1 = x - hi.astype(_F32)
    mid = r1.astype(_BF)
    lo = (r1 - mid.astype(_F32)).astype(_BF)
    return hi, mid, lo


def _dot01(m01, x):
    hi, mid, lo = _split3(x)
    return _dot(m01, hi) + _dot(m01, mid) + _dot(m01, lo)


def _dot_hi(a, b):
    ah = a.astype(_BF)
    al = (a - ah.astype(_F32)).astype(_BF)
    bh = b.astype(_BF)
    bl = (b - bh.astype(_F32)).astype(_BF)
    return _dot(ah, bh) + _dot(al, bh) + _dot(ah, bl)


def _rms(x, g, n=None):
    n = x.shape[-1] if n is None else n
    ms = jnp.sum(x * x, axis=-1, keepdims=True) * (1.0 / n)
    return x * lax.rsqrt(ms + _EPS) * g


def _log_sigmoid(x):
    return jnp.minimum(x, 0.0) - jnp.log1p(jnp.exp(-jnp.abs(x)))


def _sigmoid(x):
    return 1.0 / (1.0 + jnp.exp(-x))


def _lane_prefix(x):
    lane = lax.broadcasted_iota(jnp.int32, x.shape, 1)
    s = 1
    while s < _LANES:
        x = x + jnp.where(lane >= s, pltpu.roll(x, s, 1), 0.0)
        s *= 2
    return x


def _lane_suffix(x):
    lane = lax.broadcasted_iota(jnp.int32, x.shape, 1)
    s = 1
    while s < _LANES:
        x = x + jnp.where(lane < _LANES - s, pltpu.roll(x, _LANES - s, 1), 0.0)
        s *= 2
    return x


def _tile(n, pref, mult=_SUBLANES):
    t = min(pref, n) // mult * mult
    while n % t:
        t -= mult
    return t


def _const(shape):
    nd = len(shape)
    return pl.BlockSpec(shape, lambda *_: (0,) * nd, pipeline_mode=pl.Buffered(1))


def _params(sem):
    return pltpu.CompilerParams(dimension_semantics=sem, vmem_limit_bytes=_VMEM_LIMIT)


def _proj_body(x_ref, gat_ref, wsm_ref, wflt_ref, wflr_ref, bfc_ref, bfr_ref, gcq_ref, gckv_ref,
               wq_ref, wqr_ref, rc_ref, rs_ref, gqm_ref, gqf_ref, gkf_ref, gqe_ref,
               wuk_ref, wuv_ref, gkn_ref, gkr_ref, tri_ref,
               qmla_ref, lat_ref, krope_ref, qfox_ref, kfox_ref, vfox_ref, kfoxb_ref, vfoxb_ref,
               lft_ref, ft_ref, lfr_ref, fr_ref, qmem_ref, kmla_ref, vmla_ref, xn_ref,
               ct_ref, cr_ref, *, dims):
    ql, kvl, rd, nope, hm, fh, fd, mh, md, hp = dims
    tm = x_ref.shape[0]

    @pl.when(pl.program_id(0) == 0)
    def _():
        ct_ref[...] = jnp.zeros_like(ct_ref)
        cr_ref[...] = jnp.zeros_like(cr_ref)

    xb = _rms(x_ref[...], gat_ref[...]).astype(_BF)
    xn_ref[...] = xb
    proj = _dot(xb, wsm_ref[...])
    o = 0
    cq = proj[:, o:o + ql]; o += ql
    ckv = proj[:, o:o + kvl]; o += kvl
    krb = proj[:, o:o + _LANES]; o += _LANES
    krr = proj[:, o:o + _LANES]; o += _LANES
    fq = proj[:, o:o + fh * fd]; o += fh * fd
    fk = proj[:, o:o + fh * fd]; o += fh * fd
    fv = proj[:, o:o + fh * fd]; o += fh * fd
    mq = proj[:, o:o + mh * md]

    rc = rc_ref[...]
    rs = rs_ref[...]
    qk_dim = nope + rd

    cqn = _rms(cq, gcq_ref[...]).astype(_BF)
    q = _dot(cqn, wq_ref[...])
    qr = _dot(cqn, wqr_ref[...])
    for h in range(hm):
        sl = slice(h * hp, (h + 1) * hp)
        qh = q[:, sl] * rc + qr[:, sl] * rs
        qmla_ref[:, sl] = _rms(qh, gqm_ref[...], qk_dim).astype(_BF)

    lat = _rms(ckv, gckv_ref[...])
    lat_ref[...] = lat
    kr = krb * rc[:, nope:nope + _LANES] + krr * rs[:, nope:nope + _LANES]
    krope_ref[...] = kr[:, 0:rd]

    latb = lat.astype(_BF)
    kn = _dot(latb, wuk_ref[...])
    vmla_ref[...] = _dot(latb, wuv_ref[...]).astype(_BF)
    ssr = jnp.sum(kr * kr, axis=-1, keepdims=True)
    for h in range(hm):
        knh = kn[:, h * nope:(h + 1) * nope]
        rinv = lax.rsqrt((jnp.sum(knh * knh, axis=-1, keepdims=True) + ssr) * (1.0 / qk_dim) + _EPS)
        kmla_ref[:, h * hp:h * hp + nope] = (knh * rinv * gkn_ref[...]).astype(_BF)
        kmla_ref[:, h * hp + nope:(h + 1) * hp] = (kr * rinv * gkr_ref[...]).astype(_BF)

    for h in range(fh):
        sl = slice(h * fd, (h + 1) * fd)
        qfox_ref[:, sl] = _rms(fq[:, sl], gqf_ref[...]).astype(_BF)
        kf = _rms(fk[:, sl], gkf_ref[...])
        kfox_ref[:, sl] = kf
        kfoxb_ref[:, sl] = kf.astype(_BF)
    vfox_ref[...] = fv
    vfoxb_ref[...] = fv.astype(_BF)
    for h in range(mh):
        sl = slice(h * md, (h + 1) * md)
        qmem_ref[:, sl] = _rms(mq[:, sl], gqe_ref[...]).astype(_BF)

    row = lax.broadcasted_iota(jnp.int32, (_SUBLANES, tm), 0)
    lft = jnp.where(row < fh, _log_sigmoid(_dot_nt(wflt_ref[...], xb) + bfc_ref[:, 0:1]), 0.0)
    lft_ref[...] = lft
    carry = ct_ref[:, 0:1]
    for j in range(tm // _LANES):
        c = _lane_prefix(lft[:, j * _LANES:(j + 1) * _LANES]) + carry
        ft_ref[:, j * _LANES:(j + 1) * _LANES] = c
        carry = c[:, _LANES - 1:_LANES]
    ct_ref[...] = jnp.broadcast_to(carry, ct_ref.shape)

    lane = lax.broadcasted_iota(jnp.int32, (tm, _LANES), 1)
    lfr = jnp.where(lane < fh, _log_sigmoid(_dot(xb, wflr_ref[...]) + bfr_ref[...]), 0.0)
    lfr_ref[...] = lfr
    fr = _dot01(tri_ref[...], lfr) + cr_ref[0:1, :]
    fr_ref[...] = fr
    cr_ref[...] = jnp.broadcast_to(fr[tm - 1:tm, :], cr_ref.shape)


def _gate_body(xn_ref, wg_ref, o_ref):
    o_ref[...] = _sigmoid(_dot(xn_ref[...], wg_ref[...]))


def _flash_body(*refs, heads, dk, dv, scale, bias):
    if bias:
        q_ref, k_ref, v_ref, fr_ref, ft_ref, o_ref, m_ref, l_ref, acc_ref = refs
    else:
        q_ref, k_ref, v_ref, o_ref, m_ref, l_ref, acc_ref = refs
    qi = pl.program_id(0)
    ki = pl.program_id(1)
    tq = q_ref.shape[0]
    tk = k_ref.shape[0]

    @pl.when(ki == 0)
    def _():
        m_ref[...] = jnp.full_like(m_ref, _NEG_INF)
        l_ref[...] = jnp.zeros_like(l_ref)
        acc_ref[...] = jnp.zeros_like(acc_ref)

    def step(masked):
        if masked:
            keep = (lax.broadcasted_iota(jnp.int32, (tq, tk), 0) >= lax.broadcasted_iota(jnp.int32, (tq, tk), 1))
        qk = lambda h: _dot_nt(q_ref[:, h * dk:(h + 1) * dk], k_ref[:, h * dk:(h + 1) * dk])
        s_next = qk(0)
        for h in range(heads):
            s = s_next * scale
            if h + 1 < heads:
                s_next = qk(h + 1)
            if bias:
                s = s + fr_ref[:, h:h + 1] - ft_ref[h:h + 1, :]
            if masked:
                s = jnp.where(keep, s, _NEG_INF)
            m_prev = m_ref[h]
            m_new = jnp.maximum(m_prev, jnp.max(s, axis=-1, keepdims=True))
            alpha = jnp.exp(m_prev - m_new)
            p = jnp.exp(s - m_new)
            l_ref[h] = alpha * l_ref[h] + jnp.sum(p, axis=-1, keepdims=True)
            acc_ref[h] = alpha * acc_ref[h] + _dot(p.astype(_BF), v_ref[:, h * dv:(h + 1) * dv])
            m_ref[h] = m_new

    @pl.when(ki < qi)
    def _():
        step(False)

    @pl.when(ki == qi)
    def _():
        step(True)
        for h in range(heads):
            o_ref[:, h * dv:(h + 1) * dv] = (acc_ref[h] / l_ref[h]).astype(o_ref.dtype)


def _flash(q, k, v, fr, ft, t, heads, dk, dv, scale, tq):
    nq = t // tq
    bias = fr is not None
    kmap = lambda i, j: (jnp.minimum(i, j), 0)
    in_specs = [pl.BlockSpec((tq, heads * dk), lambda i, j: (i, 0)),
                pl.BlockSpec((tq, heads * dk), kmap),
                pl.BlockSpec((tq, heads * dv), kmap)]
    args = [q, k, v]
    if bias:
        in_specs += [pl.BlockSpec((tq, _LANES), lambda i, j: (i, 0)),
                     pl.BlockSpec((_SUBLANES, tq), lambda i, j: (0, jnp.minimum(i, j)))]
        args += [fr, ft]
    return pl.pallas_call(
        functools.partial(_flash_body, heads=heads, dk=dk, dv=dv, scale=scale, bias=bias),
        grid=(nq, nq),
        in_specs=in_specs,
        out_specs=pl.BlockSpec((tq, heads * dv), lambda i, j: (i, 0)),
        out_shape=jax.ShapeDtypeStruct((t, heads * dv), _BF),
        scratch_shapes=[pltpu.VMEM((heads, tq, 1), _F32), pltpu.VMEM((heads, tq, 1), _F32),
                        pltpu.VMEM((heads, tq, dv), _F32)],
        compiler_params=_params(("parallel", "arbitrary")),
        name="flash_bias" if bias else "flash_mla",
    )(*args)


def _memkv_body(mem_ref, g_ref, w_ref, gk_ref, k_ref, v_ref, kb_ref, vb_ref, *, heads, d):
    mn = _rms(mem_ref[...], g_ref[...]).astype(_BF)
    kv = _dot(mn, w_ref[...])
    for h in range(heads):
        sl = slice(h * d, (h + 1) * d)
        k = _rms(kv[:, sl], gk_ref[...])
        k_ref[:, sl] = k
        kb_ref[:, sl] = k.astype(_BF)
    v = kv[:, heads * d:]
    v_ref[...] = v
    vb_ref[...] = v.astype(_BF)


def _memattn_body(q_ref, k_ref, v_ref, o_ref, *, heads, d, scale):
    for h in range(heads):
        sl = slice(h * d, (h + 1) * d)
        s = _dot_nt(q_ref[:, sl], k_ref[:, sl]) * scale
        p = jnp.exp(s - jnp.max(s, axis=-1, keepdims=True))
        o = _dot(p.astype(_BF), v_ref[:, sl]) / jnp.sum(p, axis=-1, keepdims=True)
        o_ref[:, sl] = o.astype(o_ref.dtype)


def _sprep_body(q_ref, gkn_ref, gkr_ref, wukh_ref, lfr_ref, g4_ref, qhi_ref, qlo_ref, qr_ref, c_ref, *, hm, nope, rd, hp):
    for h in range(hm):
        qn = (q_ref[:, h * hp:h * hp + nope].astype(_F32) * gkn_ref[...]).astype(_BF)
        ql = _dot(qn, wukh_ref[h])
        hi = ql.astype(_BF)
        qhi_ref[h] = hi
        qlo_ref[h] = (ql - hi.astype(_F32)).astype(_BF)
        qr_ref[h] = (q_ref[:, h * hp + nope:h * hp + nope + _LANES].astype(_F32) * gkr_ref[...]).astype(_BF)
    c_ref[...] = _dot01(g4_ref[...], lfr_ref[...])


def _sample_body(pt_ref, clat_hbm, ckr_hbm, cfk_hbm, cfv_hbm, clf_hbm,
                 qlat_ref, qr_ref, qf_ref, crow_ref, ckey_ref, latn_ref, krn_ref, fkn_ref, fvn_ref,
                 wukt_ref, wuv_ref, hmask_ref, omla_ref, ofox_ref,
                 lat_buf, kr_buf, fk_buf, fv_buf, lf_buf, sem, wq_ref, nlat_ref, nfk_ref, nfv_ref,
                 mm_ref, lm_ref, am_ref, mf_ref, lf_ref, af_ref, car_ref, latb_ref,
                 *, layer, cp, nc, ns, hm, fh, ds, rd, nope, mla_scale, fox_scale):
    b = pl.program_id(0)
    nb = pl.num_programs(0)
    npg = nc * cp
    ps = _LANES
    rows_m = ds * _SUBLANES
    rows_f = fh * _SUBLANES
    nw = hm * nope
    qk_dim = nope + rd
    rid8 = lax.broadcasted_iota(jnp.int32, (_SUBLANES, 1), 0)

    def page_copies(step, slot):
        sb = step // nc
        base = sb * npg + (nc - 1 - (step - sb * nc)) * cp
        out = []
        for g in range(cp):
            page = pt_ref[base + g]
            for src, dst in ((clat_hbm, lat_buf), (ckr_hbm, kr_buf), (cfk_hbm, fk_buf), (cfv_hbm, fv_buf),
                             (clf_hbm, lf_buf)):
                out.append(pltpu.make_async_copy(src.at[layer, page], dst.at[slot, g], sem.at[slot]))
        return out

    def mla_scores(latb, krt):
        n = latb.shape[0]
        allr = _dot_nt(wq_ref[...], latb)
        ssq = jnp.zeros((_SUBLANES, n), _F32)
        for h in range(hm):
            knh = allr[h * nope:(h + 1) * nope]
            ssq = ssq + jnp.where(rid8 == h, jnp.sum(knh * knh, axis=0, keepdims=True), 0.0)
        ssr = jnp.sum(krt * krt, axis=0, keepdims=True)
        rinv = lax.rsqrt((ssq + ssr) * (1.0 / qk_dim) + _EPS)
        s = allr[nw:nw + rows_m] + allr[nw + rows_m:nw + 2 * rows_m] + _dot(qr_ref[...], krt.astype(_BF))
        return s * jnp.concatenate([rinv] * ds, axis=0) * mla_scale

    def fox_scores(k_list, bias_list):
        out = []
        for h in range(fh):
            kh = jnp.concatenate([r[pl.ds(h, ps, stride=fh), :] for r in k_list], axis=0).astype(_BF)
            bh = jnp.concatenate([bb[h:h + 1, :] for bb in bias_list], axis=1)
            qh = qf_ref[h * _SUBLANES:(h + 1) * _SUBLANES, :]
            out.append(_dot_nt(qh, kh) * fox_scale + crow_ref[h * _SUBLANES:(h + 1) * _SUBLANES, 0:1] + bh)
        return jnp.concatenate(out, axis=0)

    def softmax_update(s, m_ref, l_ref):
        m_prev = m_ref[:, 0:1]
        m_new = jnp.maximum(m_prev, jnp.max(s, axis=-1, keepdims=True))
        alpha = jnp.exp(m_prev - m_new)
        p = jnp.exp(s - m_new)
        l_ref[...] = jnp.broadcast_to(alpha * l_ref[:, 0:1] + jnp.sum(p, axis=-1, keepdims=True), l_ref.shape)
        m_ref[...] = jnp.broadcast_to(m_new, m_ref.shape)
        return alpha, p

    def fox_pv(p, v_list, off):
        n = len(v_list) * ps
        for h in range(fh):
            vh = jnp.concatenate([r[pl.ds(h, ps, stride=fh), :] for r in v_list], axis=0).astype(_BF)
            ph = p[h * _SUBLANES:(h + 1) * _SUBLANES, off:off + n].astype(_BF)
            af_ref[h * _SUBLANES:(h + 1) * _SUBLANES, :] += _dot(ph, vh)

    @pl.when(b == 0)
    def _():
        wq_ref[0:nw, :] = wukt_ref[...]
        nlat_ref[...] = jnp.zeros_like(nlat_ref)
        nfk_ref[...] = jnp.zeros_like(nfk_ref)
        nfv_ref[...] = jnp.zeros_like(nfv_ref)
        for st in range(ns - 1):
            for cpy in page_copies(st, st):
                cpy.start()

    wq_ref[nw:nw + 2 * rows_m, :] = qlat_ref[...]
    mm_ref[...] = jnp.full_like(mm_ref, _NEG_INF)
    lm_ref[...] = jnp.zeros_like(lm_ref)
    am_ref[...] = jnp.zeros_like(am_ref)
    mf_ref[...] = jnp.full_like(mf_ref, _NEG_INF)
    lf_ref[...] = jnp.zeros_like(lf_ref)
    af_ref[...] = jnp.zeros_like(af_ref)
    car_ref[...] = jnp.zeros_like(car_ref)
    nlat_ref[0:latn_ref.shape[0], :] = latn_ref[...]
    nfk_ref[0:fkn_ref.shape[0], :] = fkn_ref[...]
    nfv_ref[0:fvn_ref.shape[0], :] = fvn_ref[...]
    latb = nlat_ref[...].astype(_BF)
    key = lax.broadcasted_iota(jnp.int32, (rows_m, ps), 1)
    tok = lax.broadcasted_iota(jnp.int32, (rows_m, ps), 0) >> 3
    s = jnp.where((key < ds) & (key <= tok), mla_scores(latb, krn_ref[...]), _NEG_INF)
    alpha, p = softmax_update(s, mm_ref, lm_ref)
    am_ref[...] = alpha * am_ref[...] + _dot(p.astype(_BF), latb)
    key = lax.broadcasted_iota(jnp.int32, (rows_f, ps), 1)
    tok = lax.broadcasted_iota(jnp.int32, (rows_f, ps), 0) & (_SUBLANES - 1)
    sf = jnp.where((key < ds) & (key <= tok), fox_scores([nfk_ref], [-ckey_ref[...]]), _NEG_INF)
    alpha, p = softmax_update(sf, mf_ref, lf_ref)
    af_ref[...] = alpha * af_ref[...]
    fox_pv(p, [nfv_ref], 0)

    def chunk(c, carry_):
        step = b * nc + c
        slot = lax.rem(step, ns)
        for cpy in page_copies(step, slot):
            cpy.wait()

        @pl.when(step + ns - 1 < nb * nc)
        def _():
            for cpy in page_copies(step + ns - 1, lax.rem(step + ns - 1, ns)):
                cpy.start()

        carry = car_ref[0:fh, :]
        after = [None] * cp
        for g in reversed(range(cp)):
            x = lf_buf[slot, g]
            suf = _lane_suffix(x)
            after[g] = carry + suf - x
            carry = carry + suf[:, 0:1]
        car_ref[0:fh, :] = carry

        sm = []
        sf = []
        for g in range(0, cp, 2):
            latb = jnp.concatenate([lat_buf[slot, g], lat_buf[slot, g + 1]], axis=0).astype(_BF)
            latb_ref[g * ps:(g + 2) * ps, :] = latb
            krt = jnp.concatenate([kr_buf[slot, g], kr_buf[slot, g + 1]], axis=1)
            sm.append(mla_scores(latb, krt))
            sf.append(fox_scores([fk_buf.at[slot, g], fk_buf.at[slot, g + 1]], after[g:g + 2]))
        alpha, p = softmax_update(jnp.concatenate(sm, axis=1), mm_ref, lm_ref)
        acc = alpha * am_ref[...]
        for g in range(0, cp, 2):
            acc = acc + _dot(p[:, g * ps:(g + 2) * ps].astype(_BF), latb_ref[g * ps:(g + 2) * ps, :])
        am_ref[...] = acc
        alpha, p = softmax_update(jnp.concatenate(sf, axis=1), mf_ref, lf_ref)
        af_ref[...] = alpha * af_ref[...]
        for g in range(0, cp, 2):
            fox_pv(p, [fv_buf.at[slot, g], fv_buf.at[slot, g + 1]], g * ps)

        return carry_
    lax.fori_loop(0, nc, chunk, 0)

    accn = (am_ref[...] / lm_ref[:, 0:1]).astype(_BF)
    o = _dot(accn, wuv_ref[...])
    for t in range(ds):
        blk = jnp.where(hmask_ref[...] > 0, o[t * _SUBLANES:(t + 1) * _SUBLANES], 0.0)
        omla_ref[t:t + 1, :] = jnp.sum(blk, axis=0, keepdims=True)
    ofox_ref[...] = af_ref[...] / lf_ref[:, 0:1]


def _smem_body(q_ref, k_ref, v_ref, o_ref, *, mh, keys, scale):
    for g in range(q_ref.shape[0]):
        for h in range(mh):
            kh = k_ref[g, pl.ds(h, keys, stride=mh), :].astype(_BF)
            vh = v_ref[g, pl.ds(h, keys, stride=mh), :].astype(_BF)
            s = _dot_nt(q_ref[g, h * _SUBLANES:(h + 1) * _SUBLANES, :], kh) * scale
            p = jnp.exp(s - jnp.max(s, axis=-1, keepdims=True))
            o_ref[g, h * _SUBLANES:(h + 1) * _SUBLANES, :] = (_dot(p.astype(_BF), vh)
                                                              / jnp.sum(p, axis=-1, keepdims=True))


def _merge_body(x_ref, g_ref, om_ref, of_ref, oe_ref, wbm_ref, wbf_ref, wbe_ref, wo_ref, gffn_ref, wr_ref, br_ref,
                h_ref, hn_ref, route_ref, *, d, ng, epg):
    g = g_ref[...]
    merged = (g[:, 0:d] * _dot(om_ref[...], wbm_ref[...]) + g[:, d:2 * d] * _dot(of_ref[...], wbf_ref[...])
              + g[:, 2 * d:3 * d] * _dot(oe_ref[...], wbe_ref[...]))
    h = x_ref[...] + _dot(merged.astype(_BF), wo_ref[...])
    h_ref[...] = h
    hn = _rms(h, gffn_ref[...])
    nslab = d // _LANES
    for j in range(nslab):
        hn_ref[pl.ds(j, h.shape[0], stride=nslab), :] = hn[:, j * _LANES:(j + 1) * _LANES]
    logits = _dot_hi(hn, wr_ref[...]) + br_ref[...]
    tm = logits.shape[0]
    lane = lax.broadcasted_iota(jnp.int32, (tm, _LANES), 1).astype(_F32)
    big = float(4 * _LANES)
    isg = lane < ng
    lg = jnp.where(isg, logits, _NEG_INF)
    eg = jnp.where(isg, jnp.exp(lg - jnp.max(lg, axis=-1, keepdims=True)), 0.0)
    pg = eg / jnp.sum(eg, axis=-1, keepdims=True)
    pg_max = jnp.max(pg, axis=-1, keepdims=True)
    gidx = jnp.min(jnp.where(isg & (pg == pg_max), lane, big), axis=-1, keepdims=True)
    lo = ng + gidx * epg
    ine = (lane >= lo) & (lane < lo + epg)
    le = jnp.where(ine, logits, _NEG_INF)
    ee = jnp.where(ine, jnp.exp(le - jnp.max(le, axis=-1, keepdims=True)), 0.0)
    pe = ee / jnp.sum(ee, axis=-1, keepdims=True)
    p1 = jnp.max(jnp.where(ine, pe, -1.0), axis=-1, keepdims=True)
    i1 = jnp.min(jnp.where(ine & (pe == p1), lane, big), axis=-1, keepdims=True)
    rest = ine & (lane != i1)
    p2 = jnp.max(jnp.where(rest, pe, -1.0), axis=-1, keepdims=True)
    i2 = jnp.min(jnp.where(rest & (pe == p2), lane, big), axis=-1, keepdims=True)
    den = p1 + p2
    w1 = pg_max * p1 / den
    w2 = pg_max * p2 / den
    route_ref[...] = jnp.where(lane == 0, i1 - ng, jnp.where(lane == 1, i2 - ng,
                                                             jnp.where(lane == 2, w1, jnp.where(lane == 3, w2, 0.0))))


def _moe_body(te_ref, rt_ref, na_ref, hn_hbm, wg_ref, wu_ref, wd_ref, y_ref, xbuf, sem, *, nslab):
    i = pl.program_id(0)
    na = na_ref[0]
    tm = xbuf.shape[1] // nslab
    slot = lax.rem(i, 2)

    def row_copy(slot_, r, tok):
        src = hn_hbm.at[pl.ds(pl.multiple_of(tok * nslab, nslab), nslab)]
        return pltpu.make_async_copy(src, xbuf.at[slot_, pl.ds(r * nslab, nslab)], sem.at[slot_])

    def gather(tile, slot_):
        for r in range(tm):
            row_copy(slot_, r, rt_ref[tile * tm + r]).start()

    @pl.when((i == 0) & (na > 0))
    def _():
        gather(0, 0)

    @pl.when(i < na)
    def _():
        for r in range(tm):
            row_copy(slot, r, 0).wait()

        @pl.when(i + 1 < na)
        def _():
            gather(i + 1, 1 - slot)
        xb = jnp.concatenate([xbuf[slot, pl.ds(j, tm, stride=nslab), :] for j in range(nslab)], axis=1).astype(_BF)
        gate = _dot(xb, wg_ref[...].astype(_BF))
        up = _dot(xb, wu_ref[...].astype(_BF))
        hmid = gate * _sigmoid(gate) * up
        y = _dot(hmid.astype(_BF), wd_ref[...].astype(_BF))
        for j in range(nslab):
            y_ref[pl.ds(j, tm, stride=nslab), :] = y[:, j * _LANES:(j + 1) * _LANES]

    @pl.when(i >= na)
    def _():
        y_ref[...] = jnp.zeros_like(y_ref)


def _combine_body(pos_ref, y_hbm, h_ref, route_ref, o_ref, ybuf, sem, *, nslab):
    i = pl.program_id(0)
    nt = pl.num_programs(0)
    tm = h_ref.shape[0]
    slot = lax.rem(i, 2)

    def row_copy(slot_, k, r, src):
        src_rows = y_hbm.at[pl.ds(pl.multiple_of(src * nslab, nslab), nslab)]
        return pltpu.make_async_copy(src_rows, ybuf.at[slot_, k, pl.ds(r * nslab, nslab)], sem.at[slot_])

    def gather(tile, slot_):
        for r in range(tm):
            for k in range(2):
                row_copy(slot_, k, r, pos_ref[2 * (tile * tm + r) + k]).start()

    @pl.when(i == 0)
    def _():
        gather(0, 0)

    for r in range(tm):
        for k in range(2):
            row_copy(slot, k, r, 0).wait()

    @pl.when(i + 1 < nt)
    def _():
        gather(i + 1, 1 - slot)
    rt = route_ref[...]
    for j in range(nslab):
        sl = slice(j * _LANES, (j + 1) * _LANES)
        o_ref[:, sl] = h_ref[:, sl] + (rt[:, 2:3] * ybuf[slot, 0, pl.ds(j, tm, stride=nslab), :]
                                       + rt[:, 3:4] * ybuf[slot, 1, pl.ds(j, tm, stride=nslab), :])


def _layer(l, x_all, t, caches, page_table, mem_prompt, w):
    (cache_lat, cache_kr, cache_fk, cache_fv, cache_lf, cache_mk, cache_mv) = caches
    n, d = x_all.shape
    db, npg = page_table.shape
    ds = (n - t) // db
    ps = cache_lat.shape[2]
    kvl = cache_lat.shape[3]
    rd = cache_kr.shape[3]
    fh, fd = cache_fk.shape[3], cache_fk.shape[4]
    mh, md = cache_mk.shape[3], cache_mk.shape[4]
    mlen = cache_mk.shape[2]
    hm, nope = w["w_uk"].shape[2], w["w_uk"].shape[3]
    dv = w["w_uv"].shape[3]
    ql = w["w_uq"].shape[1]
    qk_dim = nope + rd
    hp = -(-qk_dim // _LANES) * _LANES
    ng = w["w_router_group"].shape[2]
    ne = w["w_router_expert"].shape[2]
    epg = ne // ng
    dff = w["w_e_gate"].shape[3]
    assert ps == _LANES and nope == _LANES and fd == _LANES and md == _LANES and dv == _LANES
    assert 2 * rd == _LANES and hp == 2 * _LANES and hm == _SUBLANES and ds <= _SUBLANES
    assert ng + ne <= _LANES and fh <= _SUBLANES and mh == fh

    mla_scale = float(qk_dim) ** -0.5
    fox_scale = float(fd) ** -0.5
    mem_scale = float(md) ** -0.5

    w_in = w["w_in"][l]
    widths = (ql, kvl, rd, fh * fd, fh * fd, fh * fd, fh, mh * md, 3 * d)
    offs = [0]
    for wd_ in widths:
        offs.append(offs[-1] + wd_)
    seg = lambda i: w_in[:, offs[i]:offs[i + 1]]
    half = rd // 2
    rot = jnp.concatenate([jnp.arange(half, rd), jnp.arange(0, half)])
    zpad = jnp.zeros((d, _LANES - rd), _F32)
    wsm = jnp.concatenate([seg(0), seg(1), seg(2), zpad, seg(2)[:, rot], zpad, seg(3), seg(4), seg(5), seg(7)],
                          axis=1).astype(_BF)
    wfl = seg(6)
    wflt = jnp.zeros((_SUBLANES, d), _F32).at[0:fh].set(wfl.T).astype(_BF)
    wflr = jnp.zeros((d, _LANES), _F32).at[:, 0:fh].set(wfl).astype(_BF)
    bfc = jnp.zeros((_SUBLANES, _LANES), _F32).at[0:fh, :].set(jnp.broadcast_to(w["b_f"][l][:, None], (fh, _LANES)))
    bfr = jnp.zeros((1, _LANES), _F32).at[0, 0:fh].set(w["b_f"][l])
    wgate = seg(8).astype(_BF)
    w_uq = w["w_uq"][l].reshape(ql, hm, qk_dim)
    zq = jnp.zeros((ql, hm, hp - qk_dim), _F32)
    zn = jnp.zeros((ql, hm, nope), _F32)
    wq = jnp.concatenate([w_uq, zq], axis=2).reshape(ql, hm * hp).astype(_BF)
    wqr = jnp.concatenate([zn, w_uq[:, :, nope:][:, :, rot], zq], axis=2).reshape(ql, hm * hp).astype(_BF)
    gqm = jnp.zeros((1, hp), _F32).at[0, 0:qk_dim].set(w["g_q_mla"][l])
    gkn = w["g_k_mla"][l][None, 0:nope]
    gkr = jnp.zeros((1, _LANES), _F32).at[0, 0:rd].set(w["g_k_mla"][l][nope:])
    w_uk = w["w_uk"][l]
    wuk_flat = w_uk.reshape(kvl, hm * nope).astype(_BF)
    wukt = wuk_flat.T
    wukh = jnp.transpose(w_uk, (1, 2, 0)).astype(_BF)
    wuv_flat = w["w_uv"][l].reshape(kvl, hm * dv).astype(_BF)
    row2 = lambda v: v[None, :]

    pos = jnp.concatenate([jnp.arange(t, dtype=_F32), jnp.tile(npg * ps + jnp.arange(ds, dtype=_F32), db)])
    inv = _ROPE_THETA ** (-jnp.arange(half, dtype=_F32) / half)
    ang = pos[:, None] * inv[None, :]
    cos, sin = jnp.cos(ang), jnp.sin(ang)
    zt = jnp.zeros((n, hp - qk_dim), _F32)
    rope_c = jnp.concatenate([jnp.ones((n, nope), _F32), cos, cos, zt], axis=1)
    rope_s = jnp.concatenate([jnp.zeros((n, nope), _F32), -sin, sin, zt], axis=1)

    tm = _tile(n, 256, _LANES)
    tri = jnp.tri(tm, dtype=_BF)
    dims = (ql, kvl, rd, nope, hm, fh, fd, mh, md, hp)
    tok = lambda width: pl.BlockSpec((tm, width), lambda i: (i, 0))
    lan = pl.BlockSpec((_SUBLANES, tm), lambda i: (0, i))
    sds = jax.ShapeDtypeStruct
    proj_out = pl.pallas_call(
        functools.partial(_proj_body, dims=dims),
        grid=(n // tm,),
        in_specs=[tok(d), _const((1, d)), _const(wsm.shape), _const(wflt.shape), _const(wflr.shape),
                  _const(bfc.shape), _const(bfr.shape), _const((1, ql)), _const((1, kvl)),
                  _const(wq.shape), _const(wqr.shape), tok(hp), tok(hp), _const((1, hp)),
                  _const((1, fd)), _const((1, fd)), _const((1, md)),
                  _const(wuk_flat.shape), _const(wuv_flat.shape), _const((1, nope)), _const((1, _LANES)),
                  _const(tri.shape)],
        out_specs=[tok(hm * hp), tok(kvl), tok(rd), tok(fh * fd), tok(fh * fd), tok(fh * fd), tok(fh * fd),
                   tok(fh * fd), lan, lan, tok(_LANES), tok(_LANES), tok(mh * md), tok(hm * hp), tok(hm * dv), tok(d)],
        out_shape=[sds((n, hm * hp), _BF), sds((n, kvl), _F32), sds((n, rd), _F32), sds((n, fh * fd), _BF),
                   sds((n, fh * fd), _F32), sds((n, fh * fd), _F32), sds((n, fh * fd), _BF), sds((n, fh * fd), _BF),
                   sds((_SUBLANES, n), _F32), sds((_SUBLANES, n), _F32), sds((n, _LANES), _F32), sds((n, _LANES), _F32),
                   sds((n, mh * md), _BF), sds((n, hm * hp), _BF), sds((n, hm * dv), _BF), sds((n, d), _BF)],
        scratch_shapes=[pltpu.VMEM((_SUBLANES, _LANES), _F32), pltpu.VMEM((_SUBLANES, _LANES), _F32)],
        compiler_params=_params(("arbitrary",)),
        name="token_proj",
    )(x_all, row2(w["g_attn_norm"][l]), wsm, wflt, wflr, bfc, bfr, row2(w["g_cq"][l]), row2(w["g_ckv"][l]),
      wq, wqr, rope_c, rope_s, gqm, row2(w["g_q_fox"][l]), row2(w["g_k_fox"][l]), row2(w["g_q_mem"][l]),
      wuk_flat, wuv_flat, gkn, gkr, tri)
    (qmla, lat, krope, qfox, kfox, vfox, kfoxb, vfoxb, lft, ft, lfr, fr, qmem, kmla, vmla, xnb) = proj_out

    tg = _tile(n, 512)
    tn = d
    gates = pl.pallas_call(
        _gate_body,
        grid=(3 * d // tn, n // tg),
        in_specs=[pl.BlockSpec((tg, d), lambda j, i: (i, 0)), pl.BlockSpec((d, tn), lambda j, i: (0, j))],
        out_specs=pl.BlockSpec((tg, tn), lambda j, i: (i, j)),
        out_shape=sds((n, 3 * d), _F32),
        compiler_params=_params(("parallel", "parallel")),
        name="branch_gates",
    )(xnb, wgate)

    tq = _tile(t, 512)
    o_mla_p = _flash(qmla, kmla, vmla, None, None, t, hm, hp, dv, mla_scale, tq)
    o_fox_p = _flash(qfox, kfoxb, vfoxb, fr, ft, t, fh, fd, fd, fox_scale, tq)

    wmkv = w["w_mem_kv"][l].astype(_BF)
    mem_k, mem_v, mem_kb, mem_vb = pl.pallas_call(
        functools.partial(_memkv_body, heads=mh, d=md),
        out_shape=[sds((mlen, mh * md), _F32), sds((mlen, mh * md), _F32), sds((mlen, mh * md), _BF), sds((mlen, mh * md), _BF)],
        compiler_params=pltpu.CompilerParams(vmem_limit_bytes=_VMEM_LIMIT),
        name="memory_kv",
    )(mem_prompt, row2(w["g_mem_norm"][l]), wmkv, row2(w["g_k_mem"][l]))
    o_mem_p = pl.pallas_call(
        functools.partial(_memattn_body, heads=mh, d=md, scale=mem_scale),
        grid=(t // tq,),
        in_specs=[pl.BlockSpec((tq, mh * md), lambda i: (i, 0)), _const((mlen, mh * md)), _const((mlen, mh * md))],
        out_specs=pl.BlockSpec((tq, mh * md), lambda i: (i, 0)),
        out_shape=sds((t, mh * md), _BF),
        compiler_params=_params(("parallel",)),
        name="memory_attn_prompt",
    )(qmem, mem_kb, mem_vb)

    ns = n - t
    g4 = (jnp.tri(ns, dtype=_F32) * (jnp.arange(ns)[:, None] // ds == jnp.arange(ns)[None, :] // ds)).astype(_BF)
    qhi, qlo, qrs, cnew = pl.pallas_call(
        functools.partial(_sprep_body, hm=hm, nope=nope, rd=rd, hp=hp),
        out_shape=[sds((hm, ns, kvl), _BF), sds((hm, ns, kvl), _BF), sds((hm, ns, _LANES), _BF), sds((ns, _LANES), _F32)],
        compiler_params=pltpu.CompilerParams(vmem_limit_bytes=_VMEM_LIMIT),
        name="sample_prep",
    )(qmla[t:], gkn, gkr, wukh, lfr[t:], g4)
    to_rows = lambda a: jnp.transpose(a.reshape(hm, db, ds, a.shape[-1]), (1, 2, 0, 3)).reshape(db, ds * hm, a.shape[-1])
    qlat = jnp.concatenate([to_rows(qhi), to_rows(qlo)], axis=1)
    qrr = to_rows(qrs)[:, :, 0:rd]

    def head_rows(a):
        a = jnp.transpose(a.reshape(db, ds, fh, fd), (0, 2, 1, 3))
        return jnp.pad(a, ((0, 0), (0, 0), (0, _SUBLANES - ds), (0, 0))).reshape(db, fh * _SUBLANES, fd)
    qf_rows = head_rows(qfox[t:])
    qe_rows = head_rows(qmem[t:])
    c3 = cnew[:, 0:fh].reshape(db, ds, fh)
    crow = jnp.pad(jnp.transpose(c3, (0, 2, 1)), ((0, 0), (0, 0), (0, _SUBLANES - ds))).reshape(db, fh * _SUBLANES, 1)
    crow = jnp.broadcast_to(crow, (db, fh * _SUBLANES, _LANES))
    ckey = jnp.pad(jnp.transpose(c3, (0, 2, 1)), ((0, 0), (0, 0), (0, ps - ds)))
    pad_tok = lambda a: jnp.pad(a.reshape(db, ds, -1), ((0, 0), (0, _SUBLANES - ds), (0, 0)))
    lat_new = pad_tok(lat[t:])
    kr_new = jnp.pad(jnp.transpose(krope[t:].reshape(db, ds, rd), (0, 2, 1)), ((0, 0), (0, 0), (0, ps - ds)))
    fk_new = pad_tok(kfox[t:]).reshape(db, _SUBLANES * fh, fd)
    fv_new = pad_tok(vfox[t:]).reshape(db, _SUBLANES * fh, fd)
    hmask = (jnp.arange(hm * dv)[None, :] // dv == jnp.arange(hm)[:, None]).astype(_F32)

    n_pool = cache_lat.shape[1]
    kr_t = jnp.swapaxes(cache_kr, 2, 3)
    lf_t = jnp.swapaxes(cache_lf, 2, 3)
    fk4 = cache_fk.reshape(cache_fk.shape[0], n_pool, ps * fh, fd)
    fv4 = cache_fv.reshape(cache_fv.shape[0], n_pool, ps * fh, fd)
    cp = 8 if npg % 16 == 0 else 2
    ns_slots = 3
    nc = npg // cp
    pt = page_table.reshape(-1)
    seq = lambda shape: pl.BlockSpec((None,) + shape, lambda b, pt_ref: (b,) + (0,) * len(shape))
    cst = lambda shape: pl.BlockSpec(shape, lambda b, pt_ref: (0,) * len(shape))
    hbm = pl.BlockSpec(memory_space=pl.ANY)
    rm, rf = ds * _SUBLANES, fh * _SUBLANES
    in_specs = [hbm] * 5 + [seq((2 * rm, kvl)), seq((rm, rd)), seq((rf, fd)), seq((rf, _LANES)),
                            seq((fh, ps)), seq((_SUBLANES, kvl)), seq((rd, ps)), seq((_SUBLANES * fh, fd)), seq((_SUBLANES * fh, fd)),
                            cst(wukt.shape), cst(wuv_flat.shape), cst(hmask.shape)]
    o_mla_s, o_fox_s = pl.pallas_call(
        functools.partial(_sample_body, layer=l, cp=cp, nc=nc, ns=ns_slots, hm=hm, fh=fh, ds=ds, rd=rd, nope=nope,
                          mla_scale=mla_scale, fox_scale=fox_scale),
        grid_spec=pltpu.PrefetchScalarGridSpec(
            num_scalar_prefetch=1, grid=(db,), in_specs=in_specs,
            out_specs=[seq((ds, hm * dv)), seq((rf, fd))],
            scratch_shapes=[pltpu.VMEM((ns_slots, cp, ps, kvl), _F32), pltpu.VMEM((ns_slots, cp, rd, ps), _F32),
                            pltpu.VMEM((ns_slots, cp, ps * fh, fd), _F32), pltpu.VMEM((ns_slots, cp, ps * fh, fd), _F32),
                            pltpu.VMEM((ns_slots, cp, fh, ps), _F32), pltpu.SemaphoreType.DMA((ns_slots,)),
                            pltpu.VMEM((hm * nope + 2 * rm, kvl), _BF),
                            pltpu.VMEM((ps, kvl), _F32), pltpu.VMEM((ps * fh, fd), _F32), pltpu.VMEM((ps * fh, fd), _F32),
                            pltpu.VMEM((rm, _LANES), _F32), pltpu.VMEM((rm, _LANES), _F32), pltpu.VMEM((rm, kvl), _F32),
                            pltpu.VMEM((rf, _LANES), _F32), pltpu.VMEM((rf, _LANES), _F32), pltpu.VMEM((rf, fd), _F32),
                            pltpu.VMEM((_SUBLANES, ps), _F32), pltpu.VMEM((cp * ps, kvl), _BF)]),
        out_shape=[sds((db, ds, hm * dv), _F32), sds((db, rf, fd), _F32)],
        compiler_params=_params(("arbitrary",)),
        name="sample_attn",
    )(pt, cache_lat, kr_t, fk4, fv4, lf_t,
      qlat, qrr, qf_rows, crow, ckey, lat_new, kr_new, fk_new, fv_new, wukt, wuv_flat, hmask)

    mk4 = cache_mk.reshape(cache_mk.shape[0], db, mlen * mh, md)
    mv4 = cache_mv.reshape(cache_mv.shape[0], db, mlen * mh, md)
    gs = _tile(db, 8, 1)
    o_mem_s = pl.pallas_call(
        functools.partial(_smem_body, mh=mh, keys=mlen, scale=mem_scale),
        grid=(db // gs,),
        in_specs=[pl.BlockSpec((gs, rf, md), lambda b: (b, 0, 0)),
                  pl.BlockSpec((None, gs, mlen * mh, md), lambda b: (l, b, 0, 0)),
                  pl.BlockSpec((None, gs, mlen * mh, md), lambda b: (l, b, 0, 0))],
        out_specs=pl.BlockSpec((gs, rf, md), lambda b: (b, 0, 0)),
        out_shape=sds((db, rf, md), _F32),
        compiler_params=_params(("parallel",)),
        name="memory_attn_sample",
    )(qe_rows, mk4, mv4)

    def from_head_rows(a):
        a = a.reshape(db, fh, _SUBLANES, fd)[:, :, 0:ds]
        return jnp.transpose(a, (0, 2, 1, 3)).reshape(ns, fh * fd)
    o_mla = jnp.concatenate([o_mla_p, o_mla_s.reshape(ns, hm * dv).astype(_BF)], axis=0)
    o_fox = jnp.concatenate([o_fox_p, from_head_rows(o_fox_s).astype(_BF)], axis=0)
    o_mem = jnp.concatenate([o_mem_p, from_head_rows(o_mem_s).astype(_BF)], axis=0)

    wr = jnp.zeros((d, _LANES), _F32).at[:, 0:ng].set(w["w_router_group"][l]).at[:, ng:ng + ne].set(w["w_router_expert"][l])
    br = jnp.zeros((1, _LANES), _F32).at[0, 0:ng].set(w["b_router_group"][l]).at[0, ng:ng + ne].set(w["b_router_expert"][l])
    tmm = _tile(n, 256)
    tokm = lambda width: pl.BlockSpec((tmm, width), lambda i: (i, 0))
    wbm, wbf, wbe, wo = (w["w_br_mla"][l].astype(_BF), w["w_br_fox"][l].astype(_BF), w["w_br_mem"][l].astype(_BF),
                         w["w_o"][l].astype(_BF))
    h_all, hn_all, route = pl.pallas_call(
        functools.partial(_merge_body, d=d, ng=ng, epg=epg),
        grid=(n // tmm,),
        in_specs=[tokm(d), tokm(3 * d), tokm(hm * dv), tokm(fh * fd), tokm(mh * md), _const(wbm.shape), _const(wbf.shape),
                  _const(wbe.shape), _const(wo.shape), _const((1, d)), _const(wr.shape), _const(br.shape)],
        out_specs=[tokm(d), pl.BlockSpec((tmm * (d // _LANES), _LANES), lambda i: (i, 0)), tokm(_LANES)],
        out_shape=[sds((n, d), _F32), sds((n * (d // _LANES), _LANES), _F32), sds((n, _LANES), _F32)],
        compiler_params=_params(("parallel",)),
        name="merge_router",
    )(x_all, gates, o_mla, o_fox, o_mem, wbm, wbf, wbe, wo, row2(w["g_ffn_norm"][l]), wr, br)

    te_m = _LANES
    pair_e = route[:, 0:2].astype(jnp.int32).reshape(-1)
    onehot = (pair_e[:, None] == jnp.arange(ne)[None, :]).astype(jnp.int32)
    counts = jnp.sum(onehot, axis=0)
    rank = jnp.sum((jnp.cumsum(onehot, axis=0) - 1) * onehot, axis=1)
    padded = -(-counts // te_m) * te_m
    ends = jnp.cumsum(padded)
    starts = ends - padded
    dest = starts[pair_e] + rank
    p_pad = 2 * n + ne * te_m
    n_tiles = p_pad // te_m
    row_tok = jnp.zeros((p_pad,), jnp.int32).at[dest].set(jnp.arange(2 * n, dtype=jnp.int32) // 2)
    tile_start = jnp.arange(n_tiles, dtype=jnp.int32) * te_m
    tile_e = jnp.minimum(jnp.sum((ends[None, :] <= tile_start[:, None]).astype(jnp.int32), axis=1), ne - 1)
    n_active = (ends[-1] // te_m).astype(jnp.int32).reshape(1)

    nslab = d // _LANES
    y_pairs = pl.pallas_call(
        functools.partial(_moe_body, nslab=nslab),
        grid_spec=pltpu.PrefetchScalarGridSpec(
            num_scalar_prefetch=3, grid=(n_tiles,),
            in_specs=[pl.BlockSpec(memory_space=pl.ANY),
                      pl.BlockSpec((None, None, d, dff), lambda i, te, rt, na: (l, te[i], 0, 0)),
                      pl.BlockSpec((None, None, d, dff), lambda i, te, rt, na: (l, te[i], 0, 0)),
                      pl.BlockSpec((None, None, dff, d), lambda i, te, rt, na: (l, te[i], 0, 0))],
            out_specs=pl.BlockSpec((te_m * nslab, _LANES), lambda i, te, rt, na: (i, 0)),
            scratch_shapes=[pltpu.VMEM((2, te_m * nslab, _LANES), _F32), pltpu.SemaphoreType.DMA((2,))]),
        out_shape=sds((p_pad * nslab, _LANES), _F32),
        compiler_params=_params(("arbitrary",)),
        name="experts",
    )(tile_e, row_tok, n_active, hn_all, w["w_e_gate"], w["w_e_up"], w["w_e_down"])

    tc = _tile(n, 128)
    y_all = pl.pallas_call(
        functools.partial(_combine_body, nslab=nslab),
        grid_spec=pltpu.PrefetchScalarGridSpec(
            num_scalar_prefetch=1, grid=(n // tc,),
            in_specs=[pl.BlockSpec(memory_space=pl.ANY),
                      pl.BlockSpec((tc, d), lambda i, pos: (i, 0)),
                      pl.BlockSpec((tc, _LANES), lambda i, pos: (i, 0))],
            out_specs=pl.BlockSpec((tc, d), lambda i, pos: (i, 0)),
            scratch_shapes=[pltpu.VMEM((2, 2, tc * nslab, _LANES), _F32), pltpu.SemaphoreType.DMA((2,))]),
        out_shape=sds((n, d), _F32),
        compiler_params=_params(("arbitrary",)),
        name="combine",
    )(dest.astype(jnp.int32), y_pairs, h_all, route)

    new_rows = dict(lat=lat, krope=krope, kfox=kfox, vfox=vfox, logf=lfr[:, 0:fh], mem_k=mem_k, mem_v=mem_v)
    return y_all, new_rows


def kernel(x_prompt, x_sample, cache_mla_latent, cache_mla_krope, cache_fox_k, cache_fox_v, cache_fox_logf, cache_mem_k, cache_mem_v, page_table, mem_prompt, g_attn_norm, w_in, b_f, g_cq, w_uq, g_ckv, w_uk, w_uv, g_q_mla, g_k_mla, g_q_fox, g_k_fox, g_mem_norm, w_mem_kv, g_q_mem, g_k_mem, w_br_mla, w_br_fox, w_br_mem, w_o, g_ffn_norm, w_router_group, b_router_group, w_router_expert, b_router_expert, w_e_gate, w_e_up, w_e_down):
    w = dict(g_attn_norm=g_attn_norm, w_in=w_in, b_f=b_f, g_cq=g_cq, w_uq=w_uq, g_ckv=g_ckv, w_uk=w_uk, w_uv=w_uv,
             g_q_mla=g_q_mla, g_k_mla=g_k_mla, g_q_fox=g_q_fox, g_k_fox=g_k_fox, g_mem_norm=g_mem_norm,
             w_mem_kv=w_mem_kv, g_q_mem=g_q_mem, g_k_mem=g_k_mem, w_br_mla=w_br_mla, w_br_fox=w_br_fox,
             w_br_mem=w_br_mem, w_o=w_o, g_ffn_norm=g_ffn_norm, w_router_group=w_router_group,
             b_router_group=b_router_group, w_router_expert=w_router_expert, b_router_expert=b_router_expert,
             w_e_gate=w_e_gate, w_e_up=w_e_up, w_e_down=w_e_down)
    bsz, t, d = x_prompt.shape
    db, ds, _ = x_sample.shape
    depth = w_in.shape[0]
    assert bsz == 1, "the prompt group is a single sequence"
    fh, fd = cache_fox_k.shape[3], cache_fox_k.shape[4]
    mh, md = cache_mem_k.shape[3], cache_mem_k.shape[4]
    mlen = mem_prompt.shape[1]
    x_all = jnp.concatenate([x_prompt.reshape(t, d), x_sample.reshape(db * ds, d)], axis=0)
    caches = (cache_mla_latent, cache_mla_krope, cache_fox_k, cache_fox_v, cache_fox_logf, cache_mem_k, cache_mem_v)
    rows = []
    for l in range(depth):
        x_all, r = _layer(l, x_all, t, caches, page_table, mem_prompt[0], w)
        rows.append(r)
    st = lambda key, sl, shape: jnp.stack([r[key][sl].reshape(shape) for r in rows])
    p, s = slice(0, t), slice(t, None)
    kvl = cache_mla_latent.shape[3]
    rd = cache_mla_krope.shape[3]
    return (x_all[p].reshape(1, t, d), x_all[s].reshape(db, ds, d),
            st("lat", p, (1, t, kvl)), st("krope", p, (1, t, rd)), st("kfox", p, (1, t, fh, fd)),
            st("vfox", p, (1, t, fh, fd)), st("logf", p, (1, t, fh)),
            jnp.stack([r["mem_k"].reshape(1, mlen, mh, md) for r in rows]),
            jnp.stack([r["mem_v"].reshape(1, mlen, mh, md) for r in rows]),
            st("lat", s, (db, ds, kvl)), st("krope", s, (db, ds, rd)), st("kfox", s, (db, ds, fh, fd)),
            st("vfox", s, (db, ds, fh, fd)), st("logf", s, (db, ds, fh)))
```

```python
import functools

import jax
import jax.numpy as jnp
from jax import lax
from jax.experimental import pallas as pl
from jax.experimental.pallas import tpu as pltpu

_BF = jnp.bfloat16
_F32 = jnp.float32
_EPS = 1e-6
_NEG_INF = -1e30
_ROPE_THETA = 10000.0
_LANES = 128
_SUBLANES = 8
_VMEM_LIMIT = 56 * 1024 * 1024


def _dot(a, b):
    return jnp.dot(a, b, preferred_element_type=_F32)


def _dot_nt(a, b):
    return lax.dot_general(a, b, (((1,), (1,)), ((), ())), preferred_element_type=_F32)


def _split3(x):
    hi = x.astype(_BF)
    r1 = x - hi.astype(_F32)
    mid = r1.astype(_BF)
    lo = (r1 - mid.astype(_F32)).astype(_BF)
    return hi, mid, lo


def _dot01(m01, x):
    hi, mid, lo = _split3(x)
    return _dot(m01, hi) + _dot(m01, mid) + _dot(m01, lo)


def _dot_hi(a, b):
    ah = a.astype(_BF)
    al = (a - ah.astype(_F32)).astype(_BF)
    bh = b.astype(_BF)
    bl = (b - bh.astype(_F32)).astype(_BF)
    return _dot(ah, bh) + _dot(al, bh) + _dot(ah, bl)


def _rms(x, g, n=None):
    n = x.shape[-1] if n is None else n
    ms = jnp.sum(x * x, axis=-1, keepdims=True) * (1.0 / n)
    return x * lax.rsqrt(ms + _EPS) * g


def _log_sigmoid(x):
    return jnp.minimum(x, 0.0) - jnp.log1p(jnp.exp(-jnp.abs(x)))


def _sigmoid(x):
    return 1.0 / (1.0 + jnp.exp(-x))


def _lane_prefix(x):
    lane = lax.broadcasted_iota(jnp.int32, x.shape, 1)
    s = 1
    while s < _LANES:
        x = x + jnp.where(lane >= s, pltpu.roll(x, s, 1), 0.0)
        s *= 2
    return x


def _lane_suffix(x):
    lane = lax.broadcasted_iota(jnp.int32, x.shape, 1)
    s = 1
    while s < _LANES:
        x = x + jnp.where(lane < _LANES - s, pltpu.roll(x, _LANES - s, 1), 0.0)
        s *= 2
    return x


def _tile(n, pref, mult=_SUBLANES):
    t = min(pref, n) // mult * mult
    while n % t:
        t -= mult
    return t


def _const(shape):
    nd = len(shape)
    return pl.BlockSpec(shape, lambda *_: (0,) * nd, pipeline_mode=pl.Buffered(1))


def _params(sem):
    return pltpu.CompilerParams(dimension_semantics=sem, vmem_limit_bytes=_VMEM_LIMIT)


def _proj_body(x_ref, gat_ref, wsm_ref, wflt_ref, wflr_ref, bfc_ref, bfr_ref, gcq_ref, gckv_ref,
               wq_ref, wqr_ref, rc_ref, rs_ref, gqm_ref, gqf_ref, gkf_ref, gqe_ref,
               wuk_ref, wuv_ref, gkn_ref, gkr_ref, tri_ref,
               qmla_ref, lat_ref, krope_ref, qfox_ref, kfox_ref, vfox_ref, kfoxb_ref, vfoxb_ref,
               lft_ref, ft_ref, lfr_ref, fr_ref, qmem_ref, kmla_ref, vmla_ref, xn_ref,
               ct_ref, cr_ref, *, dims):
    ql, kvl, rd, nope, hm, fh, fd, mh, md, hp = dims
    tm = x_ref.shape[0]

    @pl.when(pl.program_id(0) == 0)
    def _():
        ct_ref[...] = jnp.zeros_like(ct_ref)
        cr_ref[...] = jnp.zeros_like(cr_ref)

    xb = _rms(x_ref[...], gat_ref[...]).astype(_BF)
    xn_ref[...] = xb
    proj = _dot(xb, wsm_ref[...])
    o = 0
    cq = proj[:, o:o + ql]; o += ql
    ckv = proj[:, o:o + kvl]; o += kvl
    krb = proj[:, o:o + _LANES]; o += _LANES
    krr = proj[:, o:o + _LANES]; o += _LANES
    fq = proj[:, o:o + fh * fd]; o += fh * fd
    fk = proj[:, o:o + fh * fd]; o += fh * fd
    fv = proj[:, o:o + fh * fd]; o += fh * fd
    mq = proj[:, o:o + mh * md]

    rc = rc_ref[...]
    rs = rs_ref[...]
    qk_dim = nope + rd

    cqn = _rms(cq, gcq_ref[...]).astype(_BF)
    q = _dot(cqn, wq_ref[...])
    qr = _dot(cqn, wqr_ref[...])
    for h in range(hm):
        sl = slice(h * hp, (h + 1) * hp)
        qh = q[:, sl] * rc + qr[:, sl] * rs
        qmla_ref[:, sl] = _rms(qh, gqm_ref[...], qk_dim).astype(_BF)

    lat = _rms(ckv, gckv_ref[...])
    lat_ref[...] = lat
    kr = krb * rc[:, nope:nope + _LANES] + krr * rs[:, nope:nope + _LANES]
    krope_ref[...] = kr[:, 0:rd]

    latb = lat.astype(_BF)
    kn = _dot(latb, wuk_ref[...])
    vmla_ref[...] = _dot(latb, wuv_ref[...]).astype(_BF)
    ssr = jnp.sum(kr * kr, axis=-1, keepdims=True)
    for h in range(hm):
        knh = kn[:, h * nope:(h + 1) * nope]
        rinv = lax.rsqrt((jnp.sum(knh * knh, axis=-1, keepdims=True) + ssr) * (1.0 / qk_dim) + _EPS)
        kmla_ref[:, h * hp:h * hp + nope] = (knh * rinv * gkn_ref[...]).astype(_BF)
        kmla_ref[:, h * hp + nope:(h + 1) * hp] = (kr * rinv * gkr_ref[...]).astype(_BF)

    for h in range(fh):
        sl = slice(h * fd, (h + 1) * fd)
        qfox_ref[:, sl] = _rms(fq[:, sl], gqf_ref[...]).astype(_BF)
        kf = _rms(fk[:, sl], gkf_ref[...])
        kfox_ref[:, sl] = kf
        kfoxb_ref[:, sl] = kf.astype(_BF)
    vfox_ref[...] = fv
    vfoxb_ref[...] = fv.astype(_BF)
    for h in range(mh):
        sl = slice(h * md, (h + 1) * md)
        qmem_ref[:, sl] = _rms(mq[:, sl], gqe_ref[...]).astype(_BF)

    row = lax.broadcasted_iota(jnp.int32, (_SUBLANES, tm), 0)
    lft = jnp.where(row < fh, _log_sigmoid(_dot_nt(wflt_ref[...], xb) + bfc_ref[:, 0:1]), 0.0)
    lft_ref[...] = lft
    carry = ct_ref[:, 0:1]
    for j in range(tm // _LANES):
        c = _lane_prefix(lft[:, j * _LANES:(j + 1) * _LANES]) + carry
        ft_ref[:, j * _LANES:(j + 1) * _LANES] = c
        carry = c[:, _LANES - 1:_LANES]
    ct_ref[...] = jnp.broadcast_to(carry, ct_ref.shape)

    lane = lax.broadcasted_iota(jnp.int32, (tm, _LANES), 1)
    lfr = jnp.where(lane < fh, _log_sigmoid(_dot(xb, wflr_ref[...]) + bfr_ref[...]), 0.0)
    lfr_ref[...] = lfr
    fr = _dot01(tri_ref[...], lfr) + cr_ref[0:1, :]
    fr_ref[...] = fr
    cr_ref[...] = jnp.broadcast_to(fr[tm - 1:tm, :], cr_ref.shape)


def _gate_body(xn_ref, wg_ref, o_ref):
    o_ref[...] = _sigmoid(_dot(xn_ref[...], wg_ref[...]))


def _flash_body(*refs, heads, dk, dv, scale, bias):
    if bias:
        q_ref, k_ref, v_ref, fr_ref, ft_ref, o_ref, m_ref, l_ref, acc_ref = refs
    else:
        q_ref, k_ref, v_ref, o_ref, m_ref, l_ref, acc_ref = refs
    qi = pl.program_id(0)
    ki = pl.program_id(1)
    tq = q_ref.shape[0]
    tk = k_ref.shape[0]

    @pl.when(ki == 0)
    def _():
        m_ref[...] = jnp.full_like(m_ref, _NEG_INF)
        l_ref[...] = jnp.zeros_like(l_ref)
        acc_ref[...] = jnp.zeros_like(acc_ref)

    def step(masked):
        if masked:
            keep = (lax.broadcasted_iota(jnp.int32, (tq, tk), 0) >= lax.broadcasted_iota(jnp.int32, (tq, tk), 1))
        qk = lambda h: _dot_nt(q_ref[:, h * dk:(h + 1) * dk], k_ref[:, h * dk:(h + 1) * dk])
        s_next = qk(0)
        for h in range(heads):
            s = s_next * scale
            if h + 1 < heads:
                s_next = qk(h + 1)
            if bias:
                s = s + fr_ref[:, h:h + 1] - ft_ref[h:h + 1, :]
            if masked:
                s = jnp.where(keep, s, _NEG_INF)
            m_prev = m_ref[h]
            m_new = jnp.maximum(m_prev, jnp.max(s, axis=-1, keepdims=True))
            alpha = jnp.exp(m_prev - m_new)
            p = jnp.exp(s - m_new)
            l_ref[h] = alpha * l_ref[h] + jnp.sum(p, axis=-1, keepdims=True)
            acc_ref[h] = alpha * acc_ref[h] + _dot(p.astype(_BF), v_ref[:, h * dv:(h + 1) * dv])
            m_ref[h] = m_new

    @pl.when(ki < qi)
    def _():
        step(False)

    @pl.when(ki == qi)
    def _():
        step(True)
        for h in range(heads):
            o_ref[:, h * dv:(h + 1) * dv] = (acc_ref[h] / l_ref[h]).astype(o_ref.dtype)


def _flash(q, k, v, fr, ft, t, heads, dk, dv, scale, tq):
    nq = t // tq
    bias = fr is not None
    kmap = lambda i, j: (jnp.minimum(i, j), 0)
    in_specs = [pl.BlockSpec((tq, heads * dk), lambda i, j: (i, 0)),
                pl.BlockSpec((tq, heads * dk), kmap),
                pl.BlockSpec((tq, heads * dv), kmap)]
    args = [q, k, v]
    if bias:
        in_specs += [pl.BlockSpec((tq, _LANES), lambda i, j: (i, 0)),
                     pl.BlockSpec((_SUBLANES, tq), lambda i, j: (0, jnp.minimum(i, j)))]
        args += [fr, ft]
    return pl.pallas_call(
        functools.partial(_flash_body, heads=heads, dk=dk, dv=dv, scale=scale, bias=bias),
        grid=(nq, nq),
        in_specs=in_specs,
        out_specs=pl.BlockSpec((tq, heads * dv), lambda i, j: (i, 0)),
        out_shape=jax.ShapeDtypeStruct((t, heads * dv), _BF),
        scratch_shapes=[pltpu.VMEM((heads, tq, 1), _F32), pltpu.VMEM((heads, tq, 1), _F32),
                        pltpu.VMEM((heads, tq, dv), _F32)],
        compiler_params=_params(("parallel", "arbitrary")),
        name="flash_bias" if bias else "flash_mla",
    )(*args)


def _memkv_body(mem_ref, g_ref, w_ref, gk_ref, k_ref, v_ref, kb_ref, vb_ref, *, heads, d):
    mn = _rms(mem_ref[...], g_ref[...]).astype(_BF)
    kv = _dot(mn, w_ref[...])
    for h in range(heads):
        sl = slice(h * d, (h + 1) * d)
        k = _rms(kv[:, sl], gk_ref[...])
        k_ref[:, sl] = k
        kb_ref[:, sl] = k.astype(_BF)
    v = kv[:, heads * d:]
    v_ref[...] = v
    vb_ref[...] = v.astype(_BF)


def _memattn_body(q_ref, k_ref, v_ref, o_ref, *, heads, d, scale):
    for h in range(heads):
        sl = slice(h * d, (h + 1) * d)
        s = _dot_nt(q_ref[:, sl], k_ref[:, sl]) * scale
        p = jnp.exp(s - jnp.max(s, axis=-1, keepdims=True))
        o = _dot(p.astype(_BF), v_ref[:, sl]) / jnp.sum(p, axis=-1, keepdims=True)
        o_ref[:, sl] = o.astype(o_ref.dtype)


def _sprep_body(q_ref, gkn_ref, gkr_ref, wukh_ref, lfr_ref, g4_ref, qhi_ref, qlo_ref, qr_ref, c_ref, *, hm, nope, rd, hp):
    for h in range(hm):
        qn = (q_ref[:, h * hp:h * hp + nope].astype(_F32) * gkn_ref[...]).astype(_BF)
        ql = _dot(qn, wukh_ref[h])
        hi = ql.astype(_BF)
        qhi_ref[h] = hi
        qlo_ref[h] = (ql - hi.astype(_F32)).astype(_BF)
        qr_ref[h] = (q_ref[:, h * hp + nope:h * hp + nope + _LANES].astype(_F32) * gkr_ref[...]).astype(_BF)
    c_ref[...] = _dot01(g4_ref[...], lfr_ref[...])


def _sample_body(pt_ref, clat_hbm, ckr_hbm, cfk_hbm, cfv_hbm, clf_hbm,
                 qlat_ref, qr_ref, qf_ref, crow_ref, ckey_ref, latn_ref, krn_ref, fkn_ref, fvn_ref,
                 wukt_ref, wuv_ref, hmask_ref, omla_ref, ofox_ref,
                 lat_buf, kr_buf, fk_buf, fv_buf, lf_buf, sem, wq_ref, nlat_ref, nfk_ref, nfv_ref,
                 mm_ref, lm_ref, am_ref, mf_ref, lf_ref, af_ref, car_ref, latb_ref,
                 *, layer, cp, nc, ns, hm, fh, ds, rd, nope, mla_scale, fox_scale):
    b = pl.program_id(0)
    nb = pl.num_programs(0)
    npg = nc * cp
    ps = _LANES
    rows_m = ds * _SUBLANES
    rows_f = fh * _SUBLANES
    nw = hm * nope
    qk_dim = nope + rd
    rid8 = lax.broadcasted_iota(jnp.int32, (_SUBLANES, 1), 0)

    def page_copies(step, slot):
        sb = step // nc
        base = sb * npg + (nc - 1 - (step - sb * nc)) * cp
        out = []
        for g in range(cp):
            page = pt_ref[base + g]
            for src, dst in ((clat_hbm, lat_buf), (ckr_hbm, kr_buf), (cfk_hbm, fk_buf), (cfv_hbm, fv_buf),
                             (clf_hbm, lf_buf)):
                out.append(pltpu.make_async_copy(src.at[layer, page], dst.at[slot, g], sem.at[slot]))
        return out

    def mla_scores(latb, krt):
        n = latb.shape[0]
        allr = _dot_nt(wq_ref[...], latb)
        ssq = jnp.zeros((_SUBLANES, n), _F32)
        for h in range(hm):
            knh = allr[h * nope:(h + 1) * nope]
            ssq = ssq + jnp.where(rid8 == h, jnp.sum(knh * knh, axis=0, keepdims=True), 0.0)
        ssr = jnp.sum(krt * krt, axis=0, keepdims=True)
        rinv = lax.rsqrt((ssq + ssr) * (1.0 / qk_dim) + _EPS)
        s = allr[nw:nw + rows_m] + allr[nw + rows_m:nw + 2 * rows_m] + _dot(qr_ref[...], krt.astype(_BF))
        return s * jnp.concatenate([rinv] * ds, axis=0) * mla_scale

    def fox_scores(k_list, bias_list):
        out = []
        for h in range(fh):
            kh = jnp.concatenate([r[pl.ds(h, ps, stride=fh), :] for r in k_list], axis=0).astype(_BF)
            bh = jnp.concatenate([bb[h:h + 1, :] for bb in bias_list], axis=1)
            qh = qf_ref[h * _SUBLANES:(h + 1) * _SUBLANES, :]
            out.append(_dot_nt(qh, kh) * fox_scale + crow_ref[h * _SUBLANES:(h + 1) * _SUBLANES, 0:1] + bh)
        return jnp.concatenate(out, axis=0)

    def softmax_update(s, m_ref, l_ref):
        m_prev = m_ref[:, 0:1]
        m_new = jnp.maximum(m_prev, jnp.max(s, axis=-1, keepdims=True))
        alpha = jnp.exp(m_prev - m_new)
        p = jnp.exp(s - m_new)
        l_ref[...] = jnp.broadcast_to(alpha * l_ref[:, 0:1] + jnp.sum(p, axis=-1, keepdims=True), l_ref.shape)
        m_ref[...] = jnp.broadcast_to(m_new, m_ref.shape)
        return alpha, p

    def fox_pv(p, v_list, off):
        n = len(v_list) * ps
        for h in range(fh):
            vh = jnp.concatenate([r[pl.ds(h, ps, stride=fh), :] for r in v_list], axis=0).astype(_BF)
            ph = p[h * _SUBLANES:(h + 1) * _SUBLANES, off:off + n].astype(_BF)
            af_ref[h * _SUBLANES:(h + 1) * _SUBLANES, :] += _dot(ph, vh)

    @pl.when(b == 0)
    def _():
        wq_ref[0:nw, :] = wukt_ref[...]
        nlat_ref[...] = jnp.zeros_like(nlat_ref)
        nfk_ref[...] = jnp.zeros_like(nfk_ref)
        nfv_ref[...] = jnp.zeros_like(nfv_ref)
        for st in range(ns - 1):
            for cpy in page_copies(st, st):
                cpy.start()

    wq_ref[nw:nw + 2 * rows_m, :] = qlat_ref[...]
    mm_ref[...] = jnp.full_like(mm_ref, _NEG_INF)
    lm_ref[...] = jnp.zeros_like(lm_ref)
    am_ref[...] = jnp.zeros_like(am_ref)
    mf_ref[...] = jnp.full_like(mf_ref, _NEG_INF)
    lf_ref[...] = jnp.zeros_like(lf_ref)
    af_ref[...] = jnp.zeros_like(af_ref)
    car_ref[...] = jnp.zeros_like(car_ref)
    nlat_ref[0:latn_ref.shape[0], :] = latn_ref[...]
    nfk_ref[0:fkn_ref.shape[0], :] = fkn_ref[...]
    nfv_ref[0:fvn_ref.shape[0], :] = fvn_ref[...]
    latb = nlat_ref[...].astype(_BF)
    key = lax.broadcasted_iota(jnp.int32, (rows_m, ps), 1)
    tok = lax.broadcasted_iota(jnp.int32, (rows_m, ps), 0) >> 3
    s = jnp.where((key < ds) & (key <= tok), mla_scores(latb, krn_ref[...]), _NEG_INF)
    alpha, p = softmax_update(s, mm_ref, lm_ref)
    am_ref[...] = alpha * am_ref[...] + _dot(p.astype(_BF), latb)
    key = lax.broadcasted_iota(jnp.int32, (rows_f, ps), 1)
    tok = lax.broadcasted_iota(jnp.int32, (rows_f, ps), 0) & (_SUBLANES - 1)
    sf = jnp.where((key < ds) & (key <= tok), fox_scores([nfk_ref], [-ckey_ref[...]]), _NEG_INF)
    alpha, p = softmax_update(sf, mf_ref, lf_ref)
    af_ref[...] = alpha * af_ref[...]
    fox_pv(p, [nfv_ref], 0)

    def chunk(c, carry_):
        step = b * nc + c
        slot = lax.rem(step, ns)
        for cpy in page_copies(step, slot):
            cpy.wait()

        @pl.when(step + ns - 1 < nb * nc)
        def _():
            for cpy in page_copies(step + ns - 1, lax.rem(step + ns - 1, ns)):
                cpy.start()

        carry = car_ref[0:fh, :]
        after = [None] * cp
        for g in reversed(range(cp)):
            x = lf_buf[slot, g]
            suf = _lane_suffix(x)
            after[g] = carry + suf - x
            carry = carry + suf[:, 0:1]
        car_ref[0:fh, :] = carry

        sm = []
        sf = []
        for g in range(0, cp, 2):
            latb = jnp.concatenate([lat_buf[slot, g], lat_buf[slot, g + 1]], axis=0).astype(_BF)
            latb_ref[g * ps:(g + 2) * ps, :] = latb
            krt = jnp.concatenate([kr_buf[slot, g], kr_buf[slot, g + 1]], axis=1)
            sm.append(mla_scores(latb, krt))
            sf.append(fox_scores([fk_buf.at[slot, g], fk_buf.at[slot, g + 1]], after[g:g + 2]))
        alpha, p = softmax_update(jnp.concatenate(sm, axis=1), mm_ref, lm_ref)
        acc = alpha * am_ref[...]
        for g in range(0, cp, 2):
            acc = acc + _dot(p[:, g * ps:(g + 2) * ps].astype(_BF), latb_ref[g * ps:(g + 2) * ps, :])
        am_ref[...] = acc
        alpha, p = softmax_update(jnp.concatenate(sf, axis=1), mf_ref, lf_ref)
        af_ref[...] = alpha * af_ref[...]
        for g in range(0, cp, 2):
            fox_pv(p, [fv_buf.at[slot, g], fv_buf.at[slot, g + 1]], g * ps)

        return carry_
    lax.fori_loop(0, nc, chunk, 0)

    accn = (am_ref[...] / lm_ref[:, 0:1]).astype(_BF)
    o = _dot(accn, wuv_ref[...])
    for t in range(ds):
        blk = jnp.where(hmask_ref[...] > 0, o[t * _SUBLANES:(t + 1) * _SUBLANES], 0.0)
        omla_ref[t:t + 1, :] = jnp.sum(blk, axis=0, keepdims=True)
    ofox_ref[...] = af_ref[...] / lf_ref[:, 0:1]


def _smem_body(q_ref, k_ref, v_ref, o_ref, *, mh, keys, scale):
    for g in range(q_ref.shape[0]):
        for h in range(mh):
            kh = k_ref[g, pl.ds(h, keys, stride=mh), :].astype(_BF)
            vh = v_ref[g, pl.ds(h, keys, stride=mh), :].astype(_BF)
            s = _dot_nt(q_ref[g, h * _SUBLANES:(h + 1) * _SUBLANES, :], kh) * scale
            p = jnp.exp(s - jnp.max(s, axis=-1, keepdims=True))
            o_ref[g, h * _SUBLANES:(h + 1) * _SUBLANES, :] = (_dot(p.astype(_BF), vh)
                                                              / jnp.sum(p, axis=-1, keepdims=True))


def _merge_body(x_ref, g_ref, om_ref, of_ref, oe_ref, wbm_ref, wbf_ref, wbe_ref, wo_ref, gffn_ref, wr_ref, br_ref,
                h_ref, hn_ref, route_ref, *, d, ng, epg):
    g = g_ref[...]
    merged = (g[:, 0:d] * _dot(om_ref[...], wbm_ref[...]) + g[:, d:2 * d] * _dot(of_ref[...], wbf_ref[...])
              + g[:, 2 * d:3 * d] * _dot(oe_ref[...], wbe_ref[...]))
    h = x_ref[...] + _dot(merged.astype(_BF), wo_ref[...])
    h_ref[...] = h
    hn = _rms(h, gffn_ref[...])
    nslab = d // _LANES
    for j in range(nslab):
        hn_ref[pl.ds(j, h.shape[0], stride=nslab), :] = hn[:, j * _LANES:(j + 1) * _LANES]
    logits = _dot_hi(hn, wr_ref[...]) + br_ref[...]
    tm = logits.shape[0]
    lane = lax.broadcasted_iota(jnp.int32, (tm, _LANES), 1).astype(_F32)
    big = float(4 * _LANES)
    isg = lane < ng
    lg = jnp.where(isg, logits, _NEG_INF)
    eg = jnp.where(isg, jnp.exp(lg - jnp.max(lg, axis=-1, keepdims=True)), 0.0)
    pg = eg / jnp.sum(eg, axis=-1, keepdims=True)
    pg_max = jnp.max(pg, axis=-1, keepdims=True)
    gidx = jnp.min(jnp.where(isg & (pg == pg_max), lane, big), axis=-1, keepdims=True)
    lo = ng + gidx * epg
    ine = (lane >= lo) & (lane < lo + epg)
    le = jnp.where(ine, logits, _NEG_INF)
    ee = jnp.where(ine, jnp.exp(le - jnp.max(le, axis=-1, keepdims=True)), 0.0)
    pe = ee / jnp.sum(ee, axis=-1, keepdims=True)
    p1 = jnp.max(jnp.where(ine, pe, -1.0), axis=-1, keepdims=True)
    i1 = jnp.min(jnp.where(ine & (pe == p1), lane, big), axis=-1, keepdims=True)
    rest = ine & (lane != i1)
    p2 = jnp.max(jnp.where(rest, pe, -1.0), axis=-1, keepdims=True)
    i2 = jnp.min(jnp.where(rest & (pe == p2), lane, big), axis=-1, keepdims=True)
    den = p1 + p2
    w1 = pg_max * p1 / den
    w2 = pg_max * p2 / den
    route_ref[...] = jnp.where(lane == 0, i1 - ng, jnp.where(lane == 1, i2 - ng,
                                                             jnp.where(lane == 2, w1, jnp.where(lane == 3, w2, 0.0))))


def _moe_body(te_ref, rt_ref, na_ref, nx_ref, par_ref, hn_hbm, wg_hbm, wu_hbm, wd_hbm, y_ref,
              xbuf, sem, wgf, wuf, wdf, wsem, wgb, wub, wdb, *, nslab, layer):
    i = pl.program_id(0)
    na = na_ref[0]
    tm = xbuf.shape[1] // nslab
    slot = lax.rem(i, 2)

    def row_copy(slot_, r, tok):
        src = hn_hbm.at[pl.ds(pl.multiple_of(tok * nslab, nslab), nslab)]
        return pltpu.make_async_copy(src, xbuf.at[slot_, pl.ds(r * nslab, nslab)], sem.at[slot_])

    def gather(tile, slot_):
        for r in range(tm):
            row_copy(slot_, r, rt_ref[tile * tm + r]).start()

    def weight_copies(e, ws):
        return [pltpu.make_async_copy(src.at[layer, e], dst.at[ws], wsem.at[ws])
                for src, dst in ((wg_hbm, wgf), (wu_hbm, wuf), (wd_hbm, wdf))]

    @pl.when((i == 0) & (na > 0))
    def _():
        gather(0, 0)
        for cpy in weight_copies(te_ref[0], par_ref[te_ref[0]]):
            cpy.start()

    @pl.when(i < na)
    def _():
        e = te_ref[i]
        ws = par_ref[e]

        @pl.when((i == 0) | (te_ref[jnp.maximum(i - 1, 0)] != e))
        def _():
            for cpy in weight_copies(e, ws):
                cpy.wait()

            @pl.when(nx_ref[e] >= 0)
            def _():
                for cpy in weight_copies(nx_ref[e], 1 - ws):
                    cpy.start()
            wgb[...] = wgf[ws].astype(_BF)
            wub[...] = wuf[ws].astype(_BF)
            wdb[...] = wdf[ws].astype(_BF)

        for r in range(tm):
            row_copy(slot, r, 0).wait()

        @pl.when(i + 1 < na)
        def _():
            gather(i + 1, 1 - slot)
        xb = jnp.concatenate([xbuf[slot, pl.ds(j, tm, stride=nslab), :] for j in range(nslab)], axis=1).astype(_BF)
        gate = _dot(xb, wgb[...])
        up = _dot(xb, wub[...])
        hmid = gate * _sigmoid(gate) * up
        y = _dot(hmid.astype(_BF), wdb[...])
        for j in range(nslab):
            y_ref[pl.ds(j, tm, stride=nslab), :] = y[:, j * _LANES:(j + 1) * _LANES]

    @pl.when(i >= na)
    def _():
        y_ref[...] = jnp.zeros_like(y_ref)


def _combine_body(pos_ref, y_hbm, h_ref, route_ref, o_ref, ybuf, sem, *, nslab):
    i = pl.program_id(0)
    nt = pl.num_programs(0)
    tm = h_ref.shape[0]
    slot = lax.rem(i, 2)

    def row_copy(slot_, k, r, src):
        src_rows = y_hbm.at[pl.ds(pl.multiple_of(src * nslab, nslab), nslab)]
        return pltpu.make_async_copy(src_rows, ybuf.at[slot_, k, pl.ds(r * nslab, nslab)], sem.at[slot_])

    def gather(tile, slot_):
        for r in range(tm):
            for k in range(2):
                row_copy(slot_, k, r, pos_ref[2 * (tile * tm + r) + k]).start()

    @pl.when(i == 0)
    def _():
        gather(0, 0)

    for r in range(tm):
        for k in range(2):
            row_copy(slot, k, r, 0).wait()

    @pl.when(i + 1 < nt)
    def _():
        gather(i + 1, 1 - slot)
    rt = route_ref[...]
    for j in range(nslab):
        sl = slice(j * _LANES, (j + 1) * _LANES)
        o_ref[:, sl] = h_ref[:, sl] + (rt[:, 2:3] * ybuf[slot, 0, pl.ds(j, tm, stride=nslab), :]
                                       + rt[:, 3:4] * ybuf[slot, 1, pl.ds(j, tm, stride=nslab), :])


def _layer(l, x_all, t, caches, page_table, mem_prompt, w):
    (cache_lat, cache_kr, cache_fk, cache_fv, cache_lf, cache_mk, cache_mv) = caches
    n, d = x_all.shape
    db, npg = page_table.shape
    ds = (n - t) // db
    ps = cache_lat.shape[2]
    kvl = cache_lat.shape[3]
    rd = cache_kr.shape[3]
    fh, fd = cache_fk.shape[3], cache_fk.shape[4]
    mh, md = cache_mk.shape[3], cache_mk.shape[4]
    mlen = cache_mk.shape[2]
    hm, nope = w["w_uk"].shape[2], w["w_uk"].shape[3]
    dv = w["w_uv"].shape[3]
    ql = w["w_uq"].shape[1]
    qk_dim = nope + rd
    hp = -(-qk_dim // _LANES) * _LANES
    ng = w["w_router_group"].shape[2]
    ne = w["w_router_expert"].shape[2]
    epg = ne // ng
    dff = w["w_e_gate"].shape[3]
    assert ps == _LANES and nope == _LANES and fd == _LANES and md == _LANES and dv == _LANES
    assert 2 * rd == _LANES and hp == 2 * _LANES and hm == _SUBLANES and ds <= _SUBLANES
    assert ng + ne <= _LANES and fh <= _SUBLANES and mh == fh

    mla_scale = float(qk_dim) ** -0.5
    fox_scale = float(fd) ** -0.5
    mem_scale = float(md) ** -0.5

    w_in = w["w_in"][l]
    widths = (ql, kvl, rd, fh * fd, fh * fd, fh * fd, fh, mh * md, 3 * d)
    offs = [0]
    for wd_ in widths:
        offs.append(offs[-1] + wd_)
    seg = lambda i: w_in[:, offs[i]:offs[i + 1]]
    half = rd // 2
    rot = jnp.concatenate([jnp.arange(half, rd), jnp.arange(0, half)])
    zpad = jnp.zeros((d, _LANES - rd), _F32)
    wsm = jnp.concatenate([seg(0), seg(1), seg(2), zpad, seg(2)[:, rot], zpad, seg(3), seg(4), seg(5), seg(7)],
                          axis=1).astype(_BF)
    wfl = seg(6)
    wflt = jnp.zeros((_SUBLANES, d), _F32).at[0:fh].set(wfl.T).astype(_BF)
    wflr = jnp.zeros((d, _LANES), _F32).at[:, 0:fh].set(wfl).astype(_BF)
    bfc = jnp.zeros((_SUBLANES, _LANES), _F32).at[0:fh, :].set(jnp.broadcast_to(w["b_f"][l][:, None], (fh, _LANES)))
    bfr = jnp.zeros((1, _LANES), _F32).at[0, 0:fh].set(w["b_f"][l])
    wgate = seg(8).astype(_BF)
    w_uq = w["w_uq"][l].reshape(ql, hm, qk_dim)
    zq = jnp.zeros((ql, hm, hp - qk_dim), _F32)
    zn = jnp.zeros((ql, hm, nope), _F32)
    wq = jnp.concatenate([w_uq, zq], axis=2).reshape(ql, hm * hp).astype(_BF)
    wqr = jnp.concatenate([zn, w_uq[:, :, nope:][:, :, rot], zq], axis=2).reshape(ql, hm * hp).astype(_BF)
    gqm = jnp.zeros((1, hp), _F32).at[0, 0:qk_dim].set(w["g_q_mla"][l])
    gkn = w["g_k_mla"][l][None, 0:nope]
    gkr = jnp.zeros((1, _LANES), _F32).at[0, 0:rd].set(w["g_k_mla"][l][nope:])
    w_uk = w["w_uk"][l]
    wuk_flat = w_uk.reshape(kvl, hm * nope).astype(_BF)
    wukt = wuk_flat.T
    wukh = jnp.transpose(w_uk, (1, 2, 0)).astype(_BF)
    wuv_flat = w["w_uv"][l].reshape(kvl, hm * dv).astype(_BF)
    row2 = lambda v: v[None, :]

    pos = jnp.concatenate([jnp.arange(t, dtype=_F32), jnp.tile(npg * ps + jnp.arange(ds, dtype=_F32), db)])
    inv = _ROPE_THETA ** (-jnp.arange(half, dtype=_F32) / half)
    ang = pos[:, None] * inv[None, :]
    cos, sin = jnp.cos(ang), jnp.sin(ang)
    zt = jnp.zeros((n, hp - qk_dim), _F32)
    rope_c = jnp.concatenate([jnp.ones((n, nope), _F32), cos, cos, zt], axis=1)
    rope_s = jnp.concatenate([jnp.zeros((n, nope), _F32), -sin, sin, zt], axis=1)

    tm = _tile(n, 256, _LANES)
    tri = jnp.tri(tm, dtype=_BF)
    dims = (ql, kvl, rd, nope, hm, fh, fd, mh, md, hp)
    tok = lambda width: pl.BlockSpec((tm, width), lambda i: (i, 0))
    lan = pl.BlockSpec((_SUBLANES, tm), lambda i: (0, i))
    sds = jax.ShapeDtypeStruct
    proj_out = pl.pallas_call(
        functools.partial(_proj_body, dims=dims),
        grid=(n // tm,),
        in_specs=[tok(d), _const((1, d)), _const(wsm.shape), _const(wflt.shape), _const(wflr.shape),
                  _const(bfc.shape), _const(bfr.shape), _const((1, ql)), _const((1, kvl)),
                  _const(wq.shape), _const(wqr.shape), tok(hp), tok(hp), _const((1, hp)),
                  _const((1, fd)), _const((1, fd)), _const((1, md)),
                  _const(wuk_flat.shape), _const(wuv_flat.shape), _const((1, nope)), _const((1, _LANES)),
                  _const(tri.shape)],
        out_specs=[tok(hm * hp), tok(kvl), tok(rd), tok(fh * fd), tok(fh * fd), tok(fh * fd), tok(fh * fd),
                   tok(fh * fd), lan, lan, tok(_LANES), tok(_LANES), tok(mh * md), tok(hm * hp), tok(hm * dv), tok(d)],
        out_shape=[sds((n, hm * hp), _BF), sds((n, kvl), _F32), sds((n, rd), _F32), sds((n, fh * fd), _BF),
                   sds((n, fh * fd), _F32), sds((n, fh * fd), _F32), sds((n, fh * fd), _BF), sds((n, fh * fd), _BF),
                   sds((_SUBLANES, n), _F32), sds((_SUBLANES, n), _F32), sds((n, _LANES), _F32), sds((n, _LANES), _F32),
                   sds((n, mh * md), _BF), sds((n, hm * hp), _BF), sds((n, hm * dv), _BF), sds((n, d), _BF)],
        scratch_shapes=[pltpu.VMEM((_SUBLANES, _LANES), _F32), pltpu.VMEM((_SUBLANES, _LANES), _F32)],
        compiler_params=_params(("arbitrary",)),
        name="token_proj",
    )(x_all, row2(w["g_attn_norm"][l]), wsm, wflt, wflr, bfc, bfr, row2(w["g_cq"][l]), row2(w["g_ckv"][l]),
      wq, wqr, rope_c, rope_s, gqm, row2(w["g_q_fox"][l]), row2(w["g_k_fox"][l]), row2(w["g_q_mem"][l]),
      wuk_flat, wuv_flat, gkn, gkr, tri)
    (qmla, lat, krope, qfox, kfox, vfox, kfoxb, vfoxb, lft, ft, lfr, fr, qmem, kmla, vmla, xnb) = proj_out

    tg = _tile(n, 512)
    tn = d
    gates = pl.pallas_call(
        _gate_body,
        grid=(3 * d // tn, n // tg),
        in_specs=[pl.BlockSpec((tg, d), lambda j, i: (i, 0)), pl.BlockSpec((d, tn), lambda j, i: (0, j))],
        out_specs=pl.BlockSpec((tg, tn), lambda j, i: (i, j)),
        out_shape=sds((n, 3 * d), _F32),
        compiler_params=_params(("parallel", "parallel")),
        name="branch_gates",
    )(xnb, wgate)

    tq = _tile(t, 512)
    o_mla_p = _flash(qmla, kmla, vmla, None, None, t, hm, hp, dv, mla_scale, tq)
    o_fox_p = _flash(qfox, kfoxb, vfoxb, fr, ft, t, fh, fd, fd, fox_scale, tq)

    wmkv = w["w_mem_kv"][l].astype(_BF)
    mem_k, mem_v, mem_kb, mem_vb = pl.pallas_call(
        functools.partial(_memkv_body, heads=mh, d=md),
        out_shape=[sds((mlen, mh * md), _F32), sds((mlen, mh * md), _F32), sds((mlen, mh * md), _BF), sds((mlen, mh * md), _BF)],
        compiler_params=pltpu.CompilerParams(vmem_limit_bytes=_VMEM_LIMIT),
        name="memory_kv",
    )(mem_prompt, row2(w["g_mem_norm"][l]), wmkv, row2(w["g_k_mem"][l]))
    o_mem_p = pl.pallas_call(
        functools.partial(_memattn_body, heads=mh, d=md, scale=mem_scale),
        grid=(t // tq,),
        in_specs=[pl.BlockSpec((tq, mh * md), lambda i: (i, 0)), _const((mlen, mh * md)), _const((mlen, mh * md))],
        out_specs=pl.BlockSpec((tq, mh * md), lambda i: (i, 0)),
        out_shape=sds((t, mh * md), _BF),
        compiler_params=_params(("parallel",)),
        name="memory_attn_prompt",
    )(qmem, mem_kb, mem_vb)

    ns = n - t
    g4 = (jnp.tri(ns, dtype=_F32) * (jnp.arange(ns)[:, None] // ds == jnp.arange(ns)[None, :] // ds)).astype(_BF)
    qhi, qlo, qrs, cnew = pl.pallas_call(
        functools.partial(_sprep_body, hm=hm, nope=nope, rd=rd, hp=hp),
        out_shape=[sds((hm, ns, kvl), _BF), sds((hm, ns, kvl), _BF), sds((hm, ns, _LANES), _BF), sds((ns, _LANES), _F32)],
        compiler_params=pltpu.CompilerParams(vmem_limit_bytes=_VMEM_LIMIT),
        name="sample_prep",
    )(qmla[t:], gkn, gkr, wukh, lfr[t:], g4)
    to_rows = lambda a: jnp.transpose(a.reshape(hm, db, ds, a.shape[-1]), (1, 2, 0, 3)).reshape(db, ds * hm, a.shape[-1])
    qlat = jnp.concatenate([to_rows(qhi), to_rows(qlo)], axis=1)
    qrr = to_rows(qrs)[:, :, 0:rd]

    def head_rows(a):
        a = jnp.transpose(a.reshape(db, ds, fh, fd), (0, 2, 1, 3))
        return jnp.pad(a, ((0, 0), (0, 0), (0, _SUBLANES - ds), (0, 0))).reshape(db, fh * _SUBLANES, fd)
    qf_rows = head_rows(qfox[t:])
    qe_rows = head_rows(qmem[t:])
    c3 = cnew[:, 0:fh].reshape(db, ds, fh)
    crow = jnp.pad(jnp.transpose(c3, (0, 2, 1)), ((0, 0), (0, 0), (0, _SUBLANES - ds))).reshape(db, fh * _SUBLANES, 1)
    crow = jnp.broadcast_to(crow, (db, fh * _SUBLANES, _LANES))
    ckey = jnp.pad(jnp.transpose(c3, (0, 2, 1)), ((0, 0), (0, 0), (0, ps - ds)))
    pad_tok = lambda a: jnp.pad(a.reshape(db, ds, -1), ((0, 0), (0, _SUBLANES - ds), (0, 0)))
    lat_new = pad_tok(lat[t:])
    kr_new = jnp.pad(jnp.transpose(krope[t:].reshape(db, ds, rd), (0, 2, 1)), ((0, 0), (0, 0), (0, ps - ds)))
    fk_new = pad_tok(kfox[t:]).reshape(db, _SUBLANES * fh, fd)
    fv_new = pad_tok(vfox[t:]).reshape(db, _SUBLANES * fh, fd)
    hmask = (jnp.arange(hm * dv)[None, :] // dv == jnp.arange(hm)[:, None]).astype(_F32)

    n_pool = cache_lat.shape[1]
    kr_t = jnp.swapaxes(cache_kr, 2, 3)
    lf_t = jnp.swapaxes(cache_lf, 2, 3)
    fk4 = cache_fk.reshape(cache_fk.shape[0], n_pool, ps * fh, fd)
    fv4 = cache_fv.reshape(cache_fv.shape[0], n_pool, ps * fh, fd)
    cp = 8 if npg % 16 == 0 else 2
    ns_slots = 3
    nc = npg // cp
    pt = page_table.reshape(-1)
    seq = lambda shape: pl.BlockSpec((None,) + shape, lambda b, pt_ref: (b,) + (0,) * len(shape))
    cst = lambda shape: pl.BlockSpec(shape, lambda b, pt_ref: (0,) * len(shape))
    hbm = pl.BlockSpec(memory_space=pl.ANY)
    rm, rf = ds * _SUBLANES, fh * _SUBLANES
    in_specs = [hbm] * 5 + [seq((2 * rm, kvl)), seq((rm, rd)), seq((rf, fd)), seq((rf, _LANES)),
                            seq((fh, ps)), seq((_SUBLANES, kvl)), seq((rd, ps)), seq((_SUBLANES * fh, fd)), seq((_SUBLANES * fh, fd)),
                            cst(wukt.shape), cst(wuv_flat.shape), cst(hmask.shape)]
    o_mla_s, o_fox_s = pl.pallas_call(
        functools.partial(_sample_body, layer=l, cp=cp, nc=nc, ns=ns_slots, hm=hm, fh=fh, ds=ds, rd=rd, nope=nope,
                          mla_scale=mla_scale, fox_scale=fox_scale),
        grid_spec=pltpu.PrefetchScalarGridSpec(
            num_scalar_prefetch=1, grid=(db,), in_specs=in_specs,
            out_specs=[seq((ds, hm * dv)), seq((rf, fd))],
            scratch_shapes=[pltpu.VMEM((ns_slots, cp, ps, kvl), _F32), pltpu.VMEM((ns_slots, cp, rd, ps), _F32),
                            pltpu.VMEM((ns_slots, cp, ps * fh, fd), _F32), pltpu.VMEM((ns_slots, cp, ps * fh, fd), _F32),
                            pltpu.VMEM((ns_slots, cp, fh, ps), _F32), pltpu.SemaphoreType.DMA((ns_slots,)),
                            pltpu.VMEM((hm * nope + 2 * rm, kvl), _BF),
                            pltpu.VMEM((ps, kvl), _F32), pltpu.VMEM((ps * fh, fd), _F32), pltpu.VMEM((ps * fh, fd), _F32),
                            pltpu.VMEM((rm, _LANES), _F32), pltpu.VMEM((rm, _LANES), _F32), pltpu.VMEM((rm, kvl), _F32),
                            pltpu.VMEM((rf, _LANES), _F32), pltpu.VMEM((rf, _LANES), _F32), pltpu.VMEM((rf, fd), _F32),
                            pltpu.VMEM((_SUBLANES, ps), _F32), pltpu.VMEM((cp * ps, kvl), _BF)]),
        out_shape=[sds((db, ds, hm * dv), _F32), sds((db, rf, fd), _F32)],
        compiler_params=_params(("arbitrary",)),
        name="sample_attn",
    )(pt, cache_lat, kr_t, fk4, fv4, lf_t,
      qlat, qrr, qf_rows, crow, ckey, lat_new, kr_new, fk_new, fv_new, wukt, wuv_flat, hmask)

    mk4 = cache_mk.reshape(cache_mk.shape[0], db, mlen * mh, md)
    mv4 = cache_mv.reshape(cache_mv.shape[0], db, mlen * mh, md)
    gs = _tile(db, 8, 1)
    o_mem_s = pl.pallas_call(
        functools.partial(_smem_body, mh=mh, keys=mlen, scale=mem_scale),
        grid=(db // gs,),
        in_specs=[pl.BlockSpec((gs, rf, md), lambda b: (b, 0, 0)),
                  pl.BlockSpec((None, gs, mlen * mh, md), lambda b: (l, b, 0, 0)),
                  pl.BlockSpec((None, gs, mlen * mh, md), lambda b: (l, b, 0, 0))],
        out_specs=pl.BlockSpec((gs, rf, md), lambda b: (b, 0, 0)),
        out_shape=sds((db, rf, md), _F32),
        compiler_params=_params(("parallel",)),
        name="memory_attn_sample",
    )(qe_rows, mk4, mv4)

    def from_head_rows(a):
        a = a.reshape(db, fh, _SUBLANES, fd)[:, :, 0:ds]
        return jnp.transpose(a, (0, 2, 1, 3)).reshape(ns, fh * fd)
    o_mla = jnp.concatenate([o_mla_p, o_mla_s.reshape(ns, hm * dv).astype(_BF)], axis=0)
    o_fox = jnp.concatenate([o_fox_p, from_head_rows(o_fox_s).astype(_BF)], axis=0)
    o_mem = jnp.concatenate([o_mem_p, from_head_rows(o_mem_s).astype(_BF)], axis=0)

    wr = jnp.zeros((d, _LANES), _F32).at[:, 0:ng].set(w["w_router_group"][l]).at[:, ng:ng + ne].set(w["w_router_expert"][l])
    br = jnp.zeros((1, _LANES), _F32).at[0, 0:ng].set(w["b_router_group"][l]).at[0, ng:ng + ne].set(w["b_router_expert"][l])
    tmm = _tile(n, 256)
    tokm = lambda width: pl.BlockSpec((tmm, width), lambda i: (i, 0))
    wbm, wbf, wbe, wo = (w["w_br_mla"][l].astype(_BF), w["w_br_fox"][l].astype(_BF), w["w_br_mem"][l].astype(_BF),
                         w["w_o"][l].astype(_BF))
    h_all, hn_all, route = pl.pallas_call(
        functools.partial(_merge_body, d=d, ng=ng, epg=epg),
        grid=(n // tmm,),
        in_specs=[tokm(d), tokm(3 * d), tokm(hm * dv), tokm(fh * fd), tokm(mh * md), _const(wbm.shape), _const(wbf.shape),
                  _const(wbe.shape), _const(wo.shape), _const((1, d)), _const(wr.shape), _const(br.shape)],
        out_specs=[tokm(d), pl.BlockSpec((tmm * (d // _LANES), _LANES), lambda i: (i, 0)), tokm(_LANES)],
        out_shape=[sds((n, d), _F32), sds((n * (d // _LANES), _LANES), _F32), sds((n, _LANES), _F32)],
        compiler_params=_params(("parallel",)),
        name="merge_router",
    )(x_all, gates, o_mla, o_fox, o_mem, wbm, wbf, wbe, wo, row2(w["g_ffn_norm"][l]), wr, br)

    te_m = _LANES
    pair_e = route[:, 0:2].astype(jnp.int32).reshape(-1)
    onehot = (pair_e[:, None] == jnp.arange(ne)[None, :]).astype(jnp.int32)
    counts = jnp.sum(onehot, axis=0)
    rank = jnp.sum((jnp.cumsum(onehot, axis=0) - 1) * onehot, axis=1)
    padded = -(-counts // te_m) * te_m
    ends = jnp.cumsum(padded)
    starts = ends - padded
    dest = starts[pair_e] + rank
    p_pad = 2 * n + ne * te_m
    n_tiles = p_pad // te_m
    row_tok = jnp.zeros((p_pad,), jnp.int32).at[dest].set(jnp.arange(2 * n, dtype=jnp.int32) // 2)
    tile_start = jnp.arange(n_tiles, dtype=jnp.int32) * te_m
    tile_e = jnp.minimum(jnp.sum((ends[None, :] <= tile_start[:, None]).astype(jnp.int32), axis=1), ne - 1)
    n_active = (ends[-1] // te_m).astype(jnp.int32).reshape(1)

    nslab = d // _LANES
    has = counts > 0
    cand = jnp.where(has, jnp.arange(ne, dtype=jnp.int32), ne)
    nxt_incl = lax.cummin(cand, axis=0, reverse=True)
    nxt = jnp.concatenate([nxt_incl[1:], jnp.full((1,), ne, jnp.int32)])
    nxt = jnp.where(nxt >= ne, -1, nxt).astype(jnp.int32)
    par = ((jnp.cumsum(has.astype(jnp.int32)) - 1) & 1).astype(jnp.int32)
    hbm = pl.BlockSpec(memory_space=pl.ANY)
    y_pairs = pl.pallas_call(
        functools.partial(_moe_body, nslab=nslab, layer=l),
        grid_spec=pltpu.PrefetchScalarGridSpec(
            num_scalar_prefetch=5, grid=(n_tiles,),
            in_specs=[hbm, hbm, hbm, hbm],
            out_specs=pl.BlockSpec((te_m * nslab, _LANES), lambda i, *_: (i, 0)),
            scratch_shapes=[pltpu.VMEM((2, te_m * nslab, _LANES), _F32), pltpu.SemaphoreType.DMA((2,)),
                            pltpu.VMEM((2, d, dff), _F32), pltpu.VMEM((2, d, dff), _F32), pltpu.VMEM((2, dff, d), _F32),
                            pltpu.SemaphoreType.DMA((2,)),
                            pltpu.VMEM((d, dff), _BF), pltpu.VMEM((d, dff), _BF), pltpu.VMEM((dff, d), _BF)]),
        out_shape=sds((p_pad * nslab, _LANES), _F32),
        compiler_params=_params(("arbitrary",)),
        name="experts",
    )(tile_e, row_tok, n_active, nxt, par, hn_all, w["w_e_gate"], w["w_e_up"], w["w_e_down"])

    tc = _tile(n, 128)
    y_all = pl.pallas_call(
        functools.partial(_combine_body, nslab=nslab),
        grid_spec=pltpu.PrefetchScalarGridSpec(
            num_scalar_prefetch=1, grid=(n // tc,),
            in_specs=[pl.BlockSpec(memory_space=pl.ANY),
                      pl.BlockSpec((tc, d), lambda i, pos: (i, 0)),
                      pl.BlockSpec((tc, _LANES), lambda i, pos: (i, 0))],
            out_specs=pl.BlockSpec((tc, d), lambda i, pos: (i, 0)),
            scratch_shapes=[pltpu.VMEM((2, 2, tc * nslab, _LANES), _F32), pltpu.SemaphoreType.DMA((2,))]),
        out_shape=sds((n, d), _F32),
        compiler_params=_params(("arbitrary",)),
        name="combine",
    )(dest.astype(jnp.int32), y_pairs, h_all, route)

    new_rows = dict(lat=lat, krope=krope, kfox=kfox, vfox=vfox, logf=lfr[:, 0:fh], mem_k=mem_k, mem_v=mem_v)
    return y_all, new_rows


def kernel(x_prompt, x_sample, cache_mla_latent, cache_mla_krope, cache_fox_k, cache_fox_v, cache_fox_logf, cache_mem_k, cache_mem_v, page_table, mem_prompt, g_attn_norm, w_in, b_f, g_cq, w_uq, g_ckv, w_uk, w_uv, g_q_mla, g_k_mla, g_q_fox, g_k_fox, g_mem_norm, w_mem_kv, g_q_mem, g_k_mem, w_br_mla, w_br_fox, w_br_mem, w_o, g_ffn_norm, w_router_group, b_router_group, w_router_expert, b_router_expert, w_e_gate, w_e_up, w_e_down):
    w = dict(g_attn_norm=g_attn_norm, w_in=w_in, b_f=b_f, g_cq=g_cq, w_uq=w_uq, g_ckv=g_ckv, w_uk=w_uk, w_uv=w_uv,
             g_q_mla=g_q_mla, g_k_mla=g_k_mla, g_q_fox=g_q_fox, g_k_fox=g_k_fox, g_mem_norm=g_mem_norm,
             w_mem_kv=w_mem_kv, g_q_mem=g_q_mem, g_k_mem=g_k_mem, w_br_mla=w_br_mla, w_br_fox=w_br_fox,
             w_br_mem=w_br_mem, w_o=w_o, g_ffn_norm=g_ffn_norm, w_router_group=w_router_group,
             b_router_group=b_router_group, w_router_expert=w_router_expert, b_router_expert=b_router_expert,
             w_e_gate=w_e_gate, w_e_up=w_e_up, w_e_down=w_e_down)
    bsz, t, d = x_prompt.shape
    db, ds, _ = x_sample.shape
    depth = w_in.shape[0]
    assert bsz == 1, "the prompt group is a single sequence"
    fh, fd = cache_fox_k.shape[3], cache_fox_k.shape[4]
    mh, md = cache_mem_k.shape[3], cache_mem_k.shape[4]
    mlen = mem_prompt.shape[1]
    x_all = jnp.concatenate([x_prompt.reshape(t, d), x_sample.reshape(db * ds, d)], axis=0)
    caches = (cache_mla_latent, cache_mla_krope, cache_fox_k, cache_fox_v, cache_fox_logf, cache_mem_k, cache_mem_v)
    rows = []
    for l in range(depth):
        x_all, r = _layer(l, x_all, t, caches, page_table, mem_prompt[0], w)
        rows.append(r)
    st = lambda key, sl, shape: jnp.stack([r[key][sl].reshape(shape) for r in rows])
    p, s = slice(0, t), slice(t, None)
    kvl = cache_mla_latent.shape[3]
    rd = cache_mla_krope.shape[3]
    return (x_all[p].reshape(1, t, d), x_all[s].reshape(db, ds, d),
            st("lat", p, (1, t, kvl)), st("krope", p, (1, t, rd)), st("kfox", p, (1, t, fh, fd)),
            st("vfox", p, (1, t, fh, fd)), st("logf", p, (1, t, fh)),
            jnp.stack([r["mem_k"].reshape(1, mlen, mh, md) for r in rows]),
            jnp.stack([r["mem_v"].reshape(1, mlen, mh, md) for r in rows]),
            st("lat", s, (db, ds, kvl)), st("krope", s, (db, ds, rd)), st("kfox", s, (db, ds, fh, fd)),
            st("vfox", s, (db, ds, fh, fd)), st("logf", s, (db, ds, fh)))
```
